```python
import math
import jax, jax.numpy as jnp
from jax import lax
import numpy as np

D_MODEL = 1024
BATCH = 8
SEQ = 4096
DEPTH = 4

CHUNK = 64
D_MIX = D_MODEL
POOL_W = D_MIX // 2
POOL_WINDOWS = (2, 4, 8, 16)
N_POOL = len(POOL_WINDOWS)
POOL_GROUP = POOL_W // N_POOL
ATT_W = D_MIX - POOL_W
N_HEADS = 8
HEAD_DIM = ATT_W // N_HEADS
LEFT_CHUNKS = 8
BAND = (LEFT_CHUNKS + 1) * CHUNK
MAX_REL = 128
N_REL = 2 * MAX_REL + 1
D_FF = 2816
CONV_K = 3
PLE_DIM = 256
IN_COLS = POOL_W + 3 * ATT_W
EPS = 1e-6

kernel_name = "hybrid_pool_chunkattn_convffn_ple"


def rmsnorm(x, g):
    xf = x.astype(jnp.float32)
    y = xf * lax.rsqrt(jnp.mean(xf * xf, axis=-1, keepdims=True) + EPS)
    return (y * g.astype(jnp.float32)).astype(x.dtype)


def pool_mixer(u, w, b, scale):
    B, S, _ = u.shape
    ug = u.reshape(B, S, N_POOL, POOL_GROUP).astype(jnp.float32)
    cs = jnp.cumsum(ug, axis=1)
    cs = jnp.concatenate([jnp.zeros_like(cs[:, :1]), cs], axis=1)
    t = jnp.arange(S)
    pooled = []
    for gi, win in enumerate(POOL_WINDOWS):
        start = jnp.maximum(t + 1 - win, 0)
        cnt = (t + 1 - start).astype(jnp.float32)
        s = cs[:, t + 1, gi] - cs[:, start, gi]
        pooled.append(s / cnt[None, :, None])
    y = (jnp.stack(pooled, axis=2) - ug).astype(u.dtype)
    y = jnp.einsum('bsgc,gcd->bsgd', y, w) + b.reshape(N_POOL, POOL_GROUP)
    return y.reshape(B, S, POOL_W) * scale


def _rel_index():
    i = np.arange(CHUNK)[:, None]
    j = np.arange(BAND)[None, :]
    dist = i + LEFT_CHUNKS * CHUNK - j
    return (np.clip(dist, -MAX_REL, MAX_REL) + MAX_REL).astype(np.int32)


def chunk_attention(q, k, v, rel_bias):
    B, S, H, Dh = q.shape
    nc = S // CHUNK
    pad = LEFT_CHUNKS * CHUNK
    kp = jnp.pad(k, ((0, 0), (pad, 0), (0, 0), (0, 0)))
    vp = jnp.pad(v, ((0, 0), (pad, 0), (0, 0), (0, 0)))
    bias = rel_bias.astype(jnp.float32)[:, _rel_index()]
    qc = q.reshape(B, nc, CHUNK, H, Dh).transpose(1, 0, 2, 3, 4)
    scale = 1.0 / math.sqrt(Dh)
    band_pos = jnp.arange(BAND)

    def one_chunk(args):
        c, qb = args
        kb = lax.dynamic_slice_in_dim(kp, c * CHUNK, BAND, axis=1)
        vb = lax.dynamic_slice_in_dim(vp, c * CHUNK, BAND, axis=1)
        s = jnp.einsum('bqhd,bkhd->bhqk', qb, kb).astype(jnp.float32) * scale + bias
        valid = band_pos + c * CHUNK >= pad
        s = jnp.where(valid[None, None, None, :], s, -jnp.inf)
        pr = jax.nn.softmax(s, axis=-1).astype(vb.dtype)
        return jnp.einsum('bhqk,bkhd->bqhd', pr, vb)

    out = lax.map(one_chunk, (jnp.arange(nc), qc))
    return out.transpose(1, 0, 2, 3, 4).reshape(B, S, H * Dh)


def causal_dwconv(h, w, b):
    S = h.shape[1]
    hp = jnp.pad(h, ((0, 0), (CONV_K - 1, 0), (0, 0)))
    y = hp[:, 0:S] * w[0]
    for kk in range(1, CONV_K):
        y = y + hp[:, kk:kk + S] * w[kk]
    return y + b


def setup_inputs(seed: int = 0) -> dict:
    key = jax.random.key(seed)
    ks = jax.random.split(key, 20)
    f32 = jnp.float32
    nrm = lambda k, shape, s: (jax.random.normal(k, shape, f32) * s).astype(f32)
    return {
        "x": nrm(ks[0], (BATCH, SEQ, D_MODEL), 1.0),
        "p": nrm(ks[1], (DEPTH, BATCH, SEQ, PLE_DIM), 1.0),
        "norm_mix_g": 1.0 + nrm(ks[2], (DEPTH, D_MODEL), 0.02),
        "w_in": nrm(ks[3], (DEPTH, D_MODEL, IN_COLS), D_MODEL ** -0.5),
        "pool_w": nrm(ks[4], (DEPTH, N_POOL, POOL_GROUP, POOL_GROUP), POOL_GROUP ** -0.5),
        "pool_b": nrm(ks[5], (DEPTH, POOL_W), 0.01),
        "pool_scale": 1.0 + nrm(ks[6], (DEPTH, POOL_W), 0.1),
        "rel_bias": nrm(ks[7], (DEPTH, N_HEADS, N_REL), 0.5),
        "w_out": nrm(ks[8], (DEPTH, D_MIX, D_MODEL), 0.5 * D_MIX ** -0.5),
        "norm_ffn_g": 1.0 + nrm(ks[9], (DEPTH, D_MODEL), 0.02),
        "w_up": nrm(ks[10], (DEPTH, D_MODEL, 2 * D_FF), D_MODEL ** -0.5),
        "conv_w": nrm(ks[11], (DEPTH, CONV_K, D_FF), CONV_K ** -0.5),
        "conv_b": nrm(ks[12], (DEPTH, D_FF), 0.01),
        "w_down": nrm(ks[13], (DEPTH, D_FF, D_MODEL), 0.5 * D_FF ** -0.5),
        "norm_ple_g": 1.0 + nrm(ks[14], (DEPTH, D_MODEL), 0.02),
        "w_ple_gate": nrm(ks[15], (DEPTH, D_MODEL, D_MODEL), D_MODEL ** -0.5),
        "b_ple_gate": nrm(ks[16], (DEPTH, D_MODEL), 0.01),
        "w_ple": nrm(ks[17], (DEPTH, PLE_DIM, D_MODEL), 0.5 * PLE_DIM ** -0.5),
        "final_g": 1.0 + nrm(ks[18], (D_MODEL,), 0.02),
    }


def reference(x, p, norm_mix_g, w_in, pool_w, pool_b, pool_scale, rel_bias, w_out,
              norm_ffn_g, w_up, conv_w, conv_b, w_down, norm_ple_g, w_ple_gate,
              b_ple_gate, w_ple, final_g):
    B, S, _ = x.shape
    h = x
    for i in range(DEPTH):
        hn = rmsnorm(h, norm_mix_g[i])
        z = hn @ w_in[i]
        u = z[..., :POOL_W]
        q, k, v = jnp.split(z[..., POOL_W:], 3, axis=-1)
        q = q.reshape(B, S, N_HEADS, HEAD_DIM)
        k = k.reshape(B, S, N_HEADS, HEAD_DIM)
        v = v.reshape(B, S, N_HEADS, HEAD_DIM)
        a = pool_mixer(u, pool_w[i], pool_b[i], pool_scale[i])
        o = chunk_attention(q, k, v, rel_bias[i])
        h = h + jnp.concatenate([a, o], axis=-1) @ w_out[i]
        hn = rmsnorm(h, norm_ffn_g[i])
        up = hn @ w_up[i]
        gt, val = up[..., :D_FF], up[..., D_FF:]
        gt = causal_dwconv(gt, conv_w[i], conv_b[i])
        h = h + (jax.nn.gelu(gt, approximate=False) * val) @ w_down[i]
        hn = rmsnorm(h, norm_ple_g[i])
        gate = jax.nn.sigmoid(hn @ w_ple_gate[i] + b_ple_gate[i])
        h = h + gate * (p[i] @ w_ple[i])
    return rmsnorm(h, final_g)
```

```python
import functools
import math

import jax
import jax.numpy as jnp
import numpy as np
from jax import lax
from jax.experimental import pallas as pl
from jax.experimental.pallas import tpu as pltpu

CHUNK = 64
POOL_WINDOWS = (2, 4, 8, 16)
POOL_GROUP = 128
POOL_W = POOL_GROUP * len(POOL_WINDOWS)
N_HEADS = 8
HEAD_DIM = 64
ATT_W = N_HEADS * HEAD_DIM
LEFT_CHUNKS = 8
LEFT = LEFT_CHUNKS * CHUNK
MAX_REL = 128
CONV_K = 3
EPS = 1e-6

LANES = 128
SUBLANES = 8
TM = LEFT
QSUB = 2 * CHUNK
KWIN = QSUB + LEFT
POOL_HALO = 16
CONV_HALO = SUBLANES
FF_CHUNK = 256
NEG = -1e30
VMEM_LIMIT = 56 * 1024 * 1024

_BF16 = jnp.bfloat16
_F32 = jnp.float32


def _dot(a, b):
    return jnp.dot(a, b, preferred_element_type=_F32)


def _dot_t(a, b):
    return lax.dot_general(a, b, (((1,), (1,)), ((), ())), preferred_element_type=_F32)


def _rmsnorm(x, g):
    ms = jnp.mean(x * x, axis=-1, keepdims=True)
    return x * lax.rsqrt(ms + EPS) * g


def _resident(shape):
    return pl.BlockSpec(shape, lambda j: (0,) * len(shape), pipeline_mode=pl.Buffered(1))


def _rows(width):
    return pl.BlockSpec((TM, width), lambda j: (j, 0))


def _mix_in_kernel(tiles_per_seq, h_ref, g_ref, w_ref, pw_ref, pb_ref, ps_ref,
                   a_ref, q_ref, k_ref, v_ref, ubuf):
    tile = pl.program_id(0) % tiles_per_seq
    hn = _rmsnorm(h_ref[...], g_ref[...]).astype(_BF16)

    @pl.when(tile == 0)
    def _():
        ubuf[0:POOL_HALO, :] = jnp.zeros((POOL_HALO, POOL_W), _F32)

    @pl.when(tile != 0)
    def _():
        ubuf[0:POOL_HALO, :] = ubuf[TM:TM + POOL_HALO, :]

    ubuf[POOL_HALO:POOL_HALO + TM, :] = _dot(hn, w_ref[:, 0:POOL_W])
    q_ref[...] = (_dot(hn, w_ref[:, POOL_W:POOL_W + ATT_W]) * (1.0 / math.sqrt(HEAD_DIM))).astype(_BF16)
    k_ref[...] = _dot(hn, w_ref[:, POOL_W + ATT_W:POOL_W + 2 * ATT_W]).astype(_BF16)
    v_ref[...] = _dot(hn, w_ref[:, POOL_W + 2 * ATT_W:POOL_W + 3 * ATT_W]).astype(_BF16)

    pos = tile * TM + lax.broadcasted_iota(jnp.int32, (TM, 1), 0)
    for gi, win in enumerate(POOL_WINDOWS):
        cols = slice(gi * POOL_GROUP, (gi + 1) * POOL_GROUP)
        u = ubuf[POOL_HALO:POOL_HALO + TM, cols]
        s = u
        for d in range(1, win):
            s = s + ubuf[POOL_HALO - d:POOL_HALO - d + TM, cols]
        cnt = jnp.minimum(pos + 1, win).astype(_F32)
        y = (s / cnt - u).astype(_BF16)
        r = _dot(y, pw_ref[gi]) + pb_ref[:, cols]
        a_ref[:, cols] = (r * ps_ref[:, cols]).astype(_BF16)


def _mix_in(h, g, w_in, pool_w, pool_b, pool_scale, tiles_per_seq):
    t, d = h.shape
    out = jax.ShapeDtypeStruct((t, ATT_W), _BF16)
    return pl.pallas_call(
        functools.partial(_mix_in_kernel, tiles_per_seq),
        grid=(t // TM,),
        in_specs=[
            _rows(d),
            _resident(g.shape),
            _resident(w_in.shape),
            _resident(pool_w.shape),
            _resident(pool_b.shape),
            _resident(pool_scale.shape),
        ],
        out_specs=[_rows(POOL_W), _rows(ATT_W), _rows(ATT_W), _rows(ATT_W)],
        out_shape=[jax.ShapeDtypeStruct((t, POOL_W), _BF16), out, out, out],
        scratch_shapes=[pltpu.VMEM((POOL_HALO + TM, POOL_W), _F32)],
        compiler_params=pltpu.CompilerParams(
            dimension_semantics=("arbitrary",), vmem_limit_bytes=VMEM_LIMIT),
        name="mix_in",
    )(h, g, w_in, pool_w, pool_b, pool_scale)


def _attn_kernel(tiles_per_seq, q_ref, kp_ref, kc_ref, vp_ref, vc_ref, bias_ref, o_ref):
    has_prev = (pl.program_id(0) % tiles_per_seq) != 0
    low_head = lax.broadcasted_iota(jnp.int32, (1, LANES), 1) < HEAD_DIM
    for pair in range(N_HEADS // 2):
        cols = slice(pair * LANES, (pair + 1) * LANES)
        for sub in range(TM // QSUB):
            rows = slice(sub * QSUB, (sub + 1) * QSUB)
            n_prev = TM - sub * QSUB
            n_cur = KWIN - n_prev
            q2 = q_ref[rows, cols]
            k_prev = kp_ref[TM - n_prev:TM, cols]
            k_cur = kc_ref[0:n_cur, cols]
            v_prev = vp_ref[TM - n_prev:TM, cols]
            v_cur = vc_ref[0:n_cur, cols]
            outs = []
            for half in range(2):
                head = 2 * pair + half
                qh = jnp.where(low_head if half == 0 else ~low_head, q2, jnp.zeros_like(q2))
                s_prev = _dot_t(qh, k_prev) + bias_ref[head, :, 0:n_prev]
                s_prev = jnp.where(has_prev, s_prev, NEG)
                s_cur = _dot_t(qh, k_cur) + bias_ref[head, :, n_prev:KWIN]
                m = jnp.maximum(jnp.max(s_prev, axis=-1, keepdims=True),
                                jnp.max(s_cur, axis=-1, keepdims=True))
                e_prev = jnp.exp(s_prev - m)
                e_cur = jnp.exp(s_cur - m)
                denom = (jnp.sum(e_prev, axis=-1, keepdims=True)
                         + jnp.sum(e_cur, axis=-1, keepdims=True))
                inv = 1.0 / denom
                outs.append(_dot((e_prev * inv).astype(_BF16), v_prev)
                            + _dot((e_cur * inv).astype(_BF16), v_cur))
            o_ref[rows, cols] = jnp.where(low_head, outs[0], outs[1]).astype(_BF16)


def _attn(q, k, v, bias, tiles_per_seq):
    t, _ = q.shape

    def prev_tile(j):
        return (jnp.where(j % tiles_per_seq == 0, j, j - 1), 0)

    return pl.pallas_call(
        functools.partial(_attn_kernel, tiles_per_seq),
        grid=(t // TM,),
        in_specs=[
            _rows(ATT_W),
            pl.BlockSpec((TM, ATT_W), prev_tile),
            _rows(ATT_W),
            pl.BlockSpec((TM, ATT_W), prev_tile),
            _rows(ATT_W),
            _resident(bias.shape),
        ],
        out_specs=_rows(ATT_W),
        out_shape=jax.ShapeDtypeStruct((t, ATT_W), _BF16),
        compiler_params=pltpu.CompilerParams(
            dimension_semantics=("arbitrary",), vmem_limit_bytes=VMEM_LIMIT),
        name="attn",
    )(q, k, k, v, v, bias)


def _attn_bias(rel_bias):
    i = np.arange(QSUB)[:, None]
    j = np.arange(KWIN)[None, :]
    idx = np.clip(i + LEFT - j, -MAX_REL, MAX_REL) + MAX_REL
    qc, kc = i // CHUNK, j // CHUNK
    visible = (kc >= qc) & (kc <= qc + LEFT_CHUNKS)
    return jnp.where(visible, rel_bias.astype(_F32)[..., idx], NEG)


def _ffn_kernel(tiles_per_seq, n_chunks, final, h_ref, a_ref, o_ref, p_ref, wout_ref, gf_ref,
                wgate_ref, wval_ref, cw_ref, cb_ref, wdn_ref, gp_ref, wpg_ref, bpg_ref, wple_ref,
                fg_ref, out_ref, h1_s, hn_s, acc_s, work_s, halo_s):
    first = (pl.program_id(0) % tiles_per_seq) == 0
    h1 = (h_ref[...] + _dot(a_ref[...], wout_ref[0:POOL_W, :])
          + _dot(o_ref[...], wout_ref[POOL_W:POOL_W + ATT_W, :]))
    h1_s[...] = h1
    hn_s[...] = _rmsnorm(h1, gf_ref[...]).astype(_BF16)
    acc_s[...] = jnp.zeros_like(acc_s)

    def chunk(c, carry):
        hn = hn_s[...]
        gt = _dot(hn, wgate_ref[c])
        val = _dot(hn, wval_ref[c])

        @pl.when(first)
        def _():
            work_s[0:CONV_HALO, :] = jnp.zeros((CONV_HALO, FF_CHUNK), _F32)

        @pl.when(jnp.logical_not(first))
        def _():
            work_s[0:CONV_HALO, :] = halo_s[c]

        work_s[CONV_HALO:CONV_HALO + TM, :] = gt
        halo_s[c] = gt[TM - CONV_HALO:TM, :]
        cw = cw_ref[c]
        y = gt * cw[CONV_K - 1:CONV_K, :] + cb_ref[c]
        for kk in range(CONV_K - 1):
            shift = CONV_K - 1 - kk
            y = y + work_s[CONV_HALO - shift:CONV_HALO - shift + TM, :] * cw[kk:kk + 1, :]
        act = 0.5 * y * (1.0 + lax.erf(y * (1.0 / math.sqrt(2.0))))
        acc_s[...] += _dot((act * val).astype(_BF16), wdn_ref[c])
        return carry

    lax.fori_loop(0, n_chunks, chunk, 0)

    h2 = h1_s[...] + acc_s[...]
    hn3 = _rmsnorm(h2, gp_ref[...]).astype(_BF16)
    gate = jax.nn.sigmoid(_dot(hn3, wpg_ref[...]) + bpg_ref[...])
    h3 = h2 + gate * _dot(p_ref[...].astype(_BF16), wple_ref[...])
    out_ref[...] = _rmsnorm(h3, fg_ref[...]) if final else h3


def _ffn(h, a, o, p, w_out, g_ffn, w_gate, w_val, conv_w, conv_b, w_down, g_ple, w_pg, b_pg, w_ple,
         final_g, tiles_per_seq, final):
    t, d = h.shape
    n_chunks = w_gate.shape[0]
    residents = (w_out, g_ffn, w_gate, w_val, conv_w, conv_b, w_down, g_ple, w_pg, b_pg, w_ple, final_g)
    return pl.pallas_call(
        functools.partial(_ffn_kernel, tiles_per_seq, n_chunks, final),
        grid=(t // TM,),
        in_specs=[_rows(d), _rows(POOL_W), _rows(ATT_W), _rows(p.shape[1])]
        + [_resident(w.shape) for w in residents],
        out_specs=_rows(d),
        out_shape=jax.ShapeDtypeStruct((t, d), _F32),
        scratch_shapes=[
            pltpu.VMEM((TM, d), _F32),
            pltpu.VMEM((TM, d), _BF16),
            pltpu.VMEM((TM, d), _F32),
            pltpu.VMEM((CONV_HALO + TM, FF_CHUNK), _F32),
            pltpu.VMEM((n_chunks, CONV_HALO, FF_CHUNK), _F32),
        ],
        compiler_params=pltpu.CompilerParams(
            dimension_semantics=("arbitrary",), vmem_limit_bytes=VMEM_LIMIT),
        name="ffn",
    )(h, a, o, p, *residents)


def kernel(x, p, norm_mix_g, w_in, pool_w, pool_b, pool_scale, rel_bias, w_out, norm_ffn_g, w_up,
           conv_w, conv_b, w_down, norm_ple_g, w_ple_gate, b_ple_gate, w_ple, final_g):
    b, s, d = x.shape
    depth = w_in.shape[0]
    d_ff = w_down.shape[1]
    assert s % TM == 0 and d_ff % FF_CHUNK == 0
    assert w_in.shape[2] == POOL_W + 3 * ATT_W and w_out.shape[1] == POOL_W + ATT_W
    t = b * s
    tiles_per_seq = s // TM
    n_chunks = d_ff // FF_CHUNK

    w_in_b = w_in.astype(_BF16)
    pool_w_b = pool_w.astype(_BF16)
    w_out_b = w_out.astype(_BF16)
    w_up_b = w_up.astype(_BF16).reshape(depth, d, 2, n_chunks, FF_CHUNK).transpose(0, 2, 3, 1, 4)
    w_down_b = w_down.astype(_BF16).reshape(depth, n_chunks, FF_CHUNK, d)
    conv_w_c = conv_w.reshape(depth, CONV_K, n_chunks, FF_CHUNK).transpose(0, 2, 1, 3)
    conv_b_c = conv_b.reshape(depth, n_chunks, 1, FF_CHUNK)
    w_pg_b = w_ple_gate.astype(_BF16)
    w_ple_b = w_ple.astype(_BF16)
    bias = _attn_bias(rel_bias)
    row = lambda v: v.reshape(1, -1)

    h = x.reshape(t, d)
    for i in range(depth):
        a, q, k, v = _mix_in(h, row(norm_mix_g[i]), w_in_b[i], pool_w_b[i], row(pool_b[i]),
                             row(pool_scale[i]), tiles_per_seq)
        o = _attn(q, k, v, bias[i], tiles_per_seq)
        h = _ffn(h, a, o, p[i].reshape(t, -1), w_out_b[i], row(norm_ffn_g[i]), w_up_b[i, 0], w_up_b[i, 1],
                 conv_w_c[i], conv_b_c[i], w_down_b[i], row(norm_ple_g[i]), w_pg_b[i], row(b_ple_gate[i]),
                 w_ple_b[i], row(final_g), tiles_per_seq, final=(i == depth - 1))
    return h.reshape(b, s, d)
```

```python
import functools
import math

import jax
import jax.numpy as jnp
from jax import lax
from jax.experimental import pallas as pl
from jax.experimental.pallas import tpu as pltpu

CHUNK = 64
POOL_WINDOWS = (2, 4, 8, 16)
POOL_GROUP = 128
POOL_W = POOL_GROUP * len(POOL_WINDOWS)
N_HEADS = 8
HEAD_DIM = 64
ATT_W = N_HEADS * HEAD_DIM
LEFT_CHUNKS = 8
LEFT = LEFT_CHUNKS * CHUNK
MAX_REL = 128
CONV_K = 3
EPS = 1e-6

LANES = 128
SUBLANES = 8
TM = LEFT
QSUB = 2 * CHUNK
KWIN = QSUB + LEFT
N_PAIRS = N_HEADS * HEAD_DIM // LANES
N_SUBS = TM // QSUB
BAND = 2 * MAX_REL
NEG_TABLE = N_PAIRS
POOL_HALO = 16
CONV_HALO = SUBLANES
FF_CHUNK = 256
NEG = -1e30
VMEM_LIMIT = 56 * 1024 * 1024

_BF16 = jnp.bfloat16
_F32 = jnp.float32


def _dot(a, b):
    return jnp.dot(a, b, preferred_element_type=_F32)


def _dot_t(a, b):
    return lax.dot_general(a, b, (((1,), (1,)), ((), ())), preferred_element_type=_F32)


def _rmsnorm(x, g):
    ms = jnp.mean(x * x, axis=-1, keepdims=True)
    return x * lax.rsqrt(ms + EPS) * g


def _resident(shape):
    return pl.BlockSpec(shape, lambda j: (0,) * len(shape), pipeline_mode=pl.Buffered(1))


def _rows(width):
    return pl.BlockSpec((TM, width), lambda j: (j, 0))


def _mix_in_kernel(tiles_per_seq, h_ref, g_ref, w_ref, wkt_ref, pw_ref, pb_ref, ps_ref,
                   a_ref, q_ref, kt_ref, v_ref, ubuf):
    tile = pl.program_id(0) % tiles_per_seq
    hn = _rmsnorm(h_ref[...], g_ref[...]).astype(_BF16)

    @pl.when(tile == 0)
    def _():
        ubuf[0:POOL_HALO, :] = jnp.zeros((POOL_HALO, POOL_W), _F32)

    @pl.when(tile != 0)
    def _():
        ubuf[0:POOL_HALO, :] = ubuf[TM:TM + POOL_HALO, :]

    ubuf[POOL_HALO:POOL_HALO + TM, :] = _dot(hn, w_ref[:, 0:POOL_W])
    q_ref[...] = (_dot(hn, w_ref[:, POOL_W:POOL_W + ATT_W]) * (1.0 / math.sqrt(HEAD_DIM))).astype(_BF16)
    kt_ref[...] = _dot_t(wkt_ref[...], hn).astype(_BF16)
    v_ref[...] = _dot(hn, w_ref[:, POOL_W + 2 * ATT_W:POOL_W + 3 * ATT_W]).astype(_BF16)

    pos = tile * TM + lax.broadcasted_iota(jnp.int32, (TM, 1), 0)
    for gi, win in enumerate(POOL_WINDOWS):
        cols = slice(gi * POOL_GROUP, (gi + 1) * POOL_GROUP)
        u = ubuf[POOL_HALO:POOL_HALO + TM, cols]
        s = u
        for d in range(1, win):
            s = s + ubuf[POOL_HALO - d:POOL_HALO - d + TM, cols]
        cnt = jnp.minimum(pos + 1, win).astype(_F32)
        y = (s / cnt - u).astype(_BF16)
        r = _dot(y, pw_ref[gi]) + pb_ref[:, cols]
        a_ref[:, cols] = (r * ps_ref[:, cols]).astype(_BF16)


def _mix_in(h, g, w_in, w_kt, pool_w, pool_b, pool_scale, tiles_per_seq):
    t, d = h.shape
    out = jax.ShapeDtypeStruct((t, ATT_W), _BF16)
    return pl.pallas_call(
        functools.partial(_mix_in_kernel, tiles_per_seq),
        grid=(t // TM,),
        in_specs=[
            _rows(d),
            _resident(g.shape),
            _resident(w_in.shape),
            _resident(w_kt.shape),
            _resident(pool_w.shape),
            _resident(pool_b.shape),
            _resident(pool_scale.shape),
        ],
        out_specs=[_rows(POOL_W), _rows(ATT_W), pl.BlockSpec((ATT_W, TM), lambda j: (0, j)), _rows(ATT_W)],
        out_shape=[jax.ShapeDtypeStruct((t, POOL_W), _BF16), out,
                   jax.ShapeDtypeStruct((ATT_W, t), _BF16), out],
        scratch_shapes=[pltpu.VMEM((POOL_HALO + TM, POOL_W), _F32)],
        compiler_params=pltpu.CompilerParams(
            dimension_semantics=("arbitrary",), vmem_limit_bytes=VMEM_LIMIT),
        name="mix_in",
    )(h, g, w_in, w_kt, pool_w, pool_b, pool_scale)


def _build_bias_tables(rb_ref, tbl):
    row = lax.broadcasted_iota(jnp.int32, (QSUB, KWIN), 0)
    col = lax.broadcasted_iota(jnp.int32, (QSUB, KWIN), 1)
    first_key = row & ~(CHUNK - 1)
    visible = (col >= first_key) & (col < first_key + (LEFT_CHUNKS + 1) * CHUNK)
    brow = lax.broadcasted_iota(jnp.int32, (QSUB, BAND), 0)
    bcol = lax.broadcasted_iota(jnp.int32, (QSUB, BAND), 1)
    for head in range(N_HEADS):
        r = rb_ref[head]
        far = r[:, 0:1]
        rolled = pltpu.roll(jnp.broadcast_to(r, (QSUB, BAND)), 0, 1, stride=1, stride_axis=0)
        band = jnp.where(bcol >= brow, rolled, far)
        full = jnp.concatenate([jnp.broadcast_to(far, (QSUB, KWIN - BAND)), band], axis=1)
        tbl[head // 2, (head % 2) * QSUB:(head % 2 + 1) * QSUB, :] = jnp.where(visible, full, NEG)
    tbl[NEG_TABLE] = jnp.full((2 * QSUB, KWIN), NEG, _F32)


def _attn_kernel(tiles_per_seq, q_ref, ktp_ref, ktc_ref, vp_ref, vc_ref, rb_ref, o_ref, tbl, s_s):
    @pl.when(pl.program_id(0) == 0)
    def _():
        _build_bias_tables(rb_ref, tbl)

    has_prev = (pl.program_id(0) % tiles_per_seq) != 0
    low_head = lax.broadcasted_iota(jnp.int32, (1, LANES), 1) < HEAD_DIM
    n_blocks = N_PAIRS * N_SUBS

    def geometry(n):
        pair, sub = divmod(n, N_SUBS)
        n_prev = TM - sub * QSUB
        return pair, slice(sub * QSUB, (sub + 1) * QSUB), slice(pair * LANES, (pair + 1) * LANES), n_prev

    def scores(n):
        pair, rows, cols, n_prev = geometry(n)
        q2 = q_ref[rows, cols]
        zero = jnp.zeros_like(q2)
        qq = jnp.concatenate([jnp.where(low_head, q2, zero), jnp.where(low_head, zero, q2)], axis=0)
        prev_tbl = jnp.where(has_prev, pair, NEG_TABLE)
        s_s[n % 2, :, 0:n_prev] = _dot(qq, ktp_ref[cols, TM - n_prev:TM]) + tbl[prev_tbl, :, 0:n_prev]
        s_s[n % 2, :, n_prev:KWIN] = _dot(qq, ktc_ref[cols, 0:KWIN - n_prev]) + tbl[pair, :, n_prev:KWIN]

    def outputs(n):
        _, rows, cols, n_prev = geometry(n)
        s = s_s[n % 2]
        e = jnp.exp(s - jnp.max(s, axis=-1, keepdims=True))
        p = (e * (1.0 / jnp.sum(e, axis=-1, keepdims=True))).astype(_BF16)
        o = (_dot(p[:, 0:n_prev], vp_ref[TM - n_prev:TM, cols])
             + _dot(p[:, n_prev:KWIN], vc_ref[0:KWIN - n_prev, cols]))
        o_ref[rows, cols] = jnp.where(low_head, o[0:QSUB], o[QSUB:2 * QSUB]).astype(_BF16)

    scores(0)
    for n in range(n_blocks):
        if n + 1 < n_blocks:
            scores(n + 1)
        outputs(n)


def _attn(q, kt, v, rb, tiles_per_seq):
    t, _ = q.shape

    def prev(j):
        return jnp.where(j % tiles_per_seq == 0, j, j - 1)

    return pl.pallas_call(
        functools.partial(_attn_kernel, tiles_per_seq),
        grid=(t // TM,),
        in_specs=[
            _rows(ATT_W),
            pl.BlockSpec((ATT_W, TM), lambda j: (0, prev(j))),
            pl.BlockSpec((ATT_W, TM), lambda j: (0, j)),
            pl.BlockSpec((TM, ATT_W), lambda j: (prev(j), 0)),
            _rows(ATT_W),
            _resident(rb.shape),
        ],
        out_specs=_rows(ATT_W),
        out_shape=jax.ShapeDtypeStruct((t, ATT_W), _BF16),
        scratch_shapes=[
            pltpu.VMEM((N_PAIRS + 1, 2 * QSUB, KWIN), _F32),
            pltpu.VMEM((2, 2 * QSUB, KWIN), _F32),
        ],
        compiler_params=pltpu.CompilerParams(
            dimension_semantics=("arbitrary",), vmem_limit_bytes=VMEM_LIMIT),
        name="attn",
    )(q, kt, kt, v, v, rb)


def _ffn_kernel(tiles_per_seq, n_chunks, final, h_ref, a_ref, o_ref, p_ref, wout_ref, gf_ref,
                wgate_ref, wval_ref, cw_ref, cb_ref, wdn_ref, gp_ref, wpg_ref, bpg_ref, wple_ref,
                fg_ref, out_ref, h1_s, hn_s, acc_s, work_s, halo_s):
    first = (pl.program_id(0) % tiles_per_seq) == 0
    h1 = (h_ref[...] + _dot(a_ref[...], wout_ref[0:POOL_W, :])
          + _dot(o_ref[...], wout_ref[POOL_W:POOL_W + ATT_W, :]))
    h1_s[...] = h1
    hn_s[...] = _rmsnorm(h1, gf_ref[...]).astype(_BF16)
    acc_s[...] = jnp.zeros_like(acc_s)

    def chunk(c, carry):
        hn = hn_s[...]
        gt = _dot(hn, wgate_ref[c])
        val = _dot(hn, wval_ref[c])

        @pl.when(first)
        def _():
            work_s[0:CONV_HALO, :] = jnp.zeros((CONV_HALO, FF_CHUNK), _F32)

        @pl.when(jnp.logical_not(first))
        def _():
            work_s[0:CONV_HALO, :] = halo_s[c]

        work_s[CONV_HALO:CONV_HALO + TM, :] = gt
        halo_s[c] = gt[TM - CONV_HALO:TM, :]
        cw = cw_ref[c]
        y = gt * cw[CONV_K - 1:CONV_K, :] + cb_ref[c]
        for kk in range(CONV_K - 1):
            shift = CONV_K - 1 - kk
            y = y + work_s[CONV_HALO - shift:CONV_HALO - shift + TM, :] * cw[kk:kk + 1, :]
        act = 0.5 * y * (1.0 + lax.erf(y * (1.0 / math.sqrt(2.0))))
        acc_s[...] += _dot((act * val).astype(_BF16), wdn_ref[c])
        return carry

    lax.fori_loop(0, n_chunks, chunk, 0)

    h2 = h1_s[...] + acc_s[...]
    hn3 = _rmsnorm(h2, gp_ref[...]).astype(_BF16)
    gate = jax.nn.sigmoid(_dot(hn3, wpg_ref[...]) + bpg_ref[...])
    h3 = h2 + gate * _dot(p_ref[...].astype(_BF16), wple_ref[...])
    out_ref[...] = _rmsnorm(h3, fg_ref[...]) if final else h3


def _ffn(h, a, o, p, w_out, g_ffn, w_gate, w_val, conv_w, conv_b, w_down, g_ple, w_pg, b_pg, w_ple,
         final_g, tiles_per_seq, final):
    t, d = h.shape
    n_chunks = w_gate.shape[0]
    residents = (w_out, g_ffn, w_gate, w_val, conv_w, conv_b, w_down, g_ple, w_pg, b_pg, w_ple, final_g)
    return pl.pallas_call(
        functools.partial(_ffn_kernel, tiles_per_seq, n_chunks, final),
        grid=(t // TM,),
        in_specs=[_rows(d), _rows(POOL_W), _rows(ATT_W), _rows(p.shape[1])]
        + [_resident(w.shape) for w in residents],
        out_specs=_rows(d),
        out_shape=jax.ShapeDtypeStruct((t, d), _F32),
        scratch_shapes=[
            pltpu.VMEM((TM, d), _F32),
            pltpu.VMEM((TM, d), _BF16),
            pltpu.VMEM((TM, d), _F32),
            pltpu.VMEM((CONV_HALO + TM, FF_CHUNK), _F32),
            pltpu.VMEM((n_chunks, CONV_HALO, FF_CHUNK), _F32),
        ],
        compiler_params=pltpu.CompilerParams(
            dimension_semantics=("arbitrary",), vmem_limit_bytes=VMEM_LIMIT),
        name="ffn",
    )(h, a, o, p, *residents)


def kernel(x, p, norm_mix_g, w_in, pool_w, pool_b, pool_scale, rel_bias, w_out, norm_ffn_g, w_up,
           conv_w, conv_b, w_down, norm_ple_g, w_ple_gate, b_ple_gate, w_ple, final_g):
    b, s, d = x.shape
    depth = w_in.shape[0]
    d_ff = w_down.shape[1]
    assert s % TM == 0 and d_ff % FF_CHUNK == 0
    assert w_in.shape[2] == POOL_W + 3 * ATT_W and w_out.shape[1] == POOL_W + ATT_W
    assert rel_bias.shape[1:] == (N_HEADS, 2 * MAX_REL + 1)
    t = b * s
    tiles_per_seq = s // TM
    n_chunks = d_ff // FF_CHUNK

    w_in_b = w_in.astype(_BF16)
    w_kt_b = w_in_b[:, :, POOL_W + ATT_W:POOL_W + 2 * ATT_W].transpose(0, 2, 1)
    pool_w_b = pool_w.astype(_BF16)
    w_out_b = w_out.astype(_BF16)
    w_up_b = w_up.astype(_BF16).reshape(depth, d, 2, n_chunks, FF_CHUNK).transpose(0, 2, 3, 1, 4)
    w_down_b = w_down.astype(_BF16).reshape(depth, n_chunks, FF_CHUNK, d)
    conv_w_c = conv_w.reshape(depth, CONV_K, n_chunks, FF_CHUNK).transpose(0, 2, 1, 3)
    conv_b_c = conv_b.reshape(depth, n_chunks, 1, FF_CHUNK)
    w_pg_b = w_ple_gate.astype(_BF16)
    w_ple_b = w_ple.astype(_BF16)
    rb = rel_bias.astype(_F32)[:, :, :0:-1].reshape(depth, N_HEADS, 1, BAND)
    row = lambda v: v.reshape(1, -1)

    h = x.reshape(t, d)
    for i in range(depth):
        a, q, kt, v = _mix_in(h, row(norm_mix_g[i]), w_in_b[i], w_kt_b[i], pool_w_b[i], row(pool_b[i]),
                              row(pool_scale[i]), tiles_per_seq)
        o = _attn(q, kt, v, rb[i], tiles_per_seq)
        h = _ffn(h, a, o, p[i].reshape(t, -1), w_out_b[i], row(norm_ffn_g[i]), w_up_b[i, 0], w_up_b[i, 1],
                 conv_w_c[i], conv_b_c[i], w_down_b[i], row(norm_ple_g[i]), w_pg_b[i], row(b_ple_gate[i]),
                 w_ple_b[i], row(final_g), tiles_per_seq, final=(i == depth - 1))
    return h.reshape(b, s, d)
```

```python
import functools
import math

import jax
import jax.numpy as jnp
from jax import lax
from jax.experimental import pallas as pl
from jax.experimental.pallas import tpu as pltpu

CHUNK = 64
POOL_WINDOWS = (2, 4, 8, 16)
POOL_GROUP = 128
POOL_W = POOL_GROUP * len(POOL_WINDOWS)
N_HEADS = 8
HEAD_DIM = 64
ATT_W = N_HEADS * HEAD_DIM
LEFT_CHUNKS = 8
LEFT = LEFT_CHUNKS * CHUNK
MAX_REL = 128
CONV_K = 3
EPS = 1e-6

LANES = 128
SUBLANES = 8
TM = LEFT
QSUB = 2 * CHUNK
KWIN = QSUB + LEFT
N_PAIRS = N_HEADS * HEAD_DIM // LANES
N_SUBS = TM // QSUB
BAND = 2 * MAX_REL
NEG_TABLE = N_PAIRS
POOL_HALO = 16
CONV_HALO = SUBLANES
FF_CHUNK = 768
NEG = -1e30
VMEM_LIMIT = 56 * 1024 * 1024

_BF16 = jnp.bfloat16
_F32 = jnp.float32


def _dot(a, b):
    return jnp.dot(a, b, preferred_element_type=_F32)


def _dot_t(a, b):
    return lax.dot_general(a, b, (((1,), (1,)), ((), ())), preferred_element_type=_F32)


def _rmsnorm(x, g):
    ms = jnp.mean(x * x, axis=-1, keepdims=True)
    return x * lax.rsqrt(ms + EPS) * g


def _resident(shape):
    return pl.BlockSpec(shape, lambda j: (0,) * len(shape), pipeline_mode=pl.Buffered(1))


def _rows(width):
    return pl.BlockSpec((TM, width), lambda j: (j, 0))


def _mix_in_kernel(tiles_per_seq, h_ref, g_ref, w_ref, wkt_ref, pw_ref, pb_ref, ps_ref,
                   a_ref, q_ref, kt_ref, v_ref, ubuf):
    tile = pl.program_id(0) % tiles_per_seq
    hn = _rmsnorm(h_ref[...], g_ref[...]).astype(_BF16)

    @pl.when(tile == 0)
    def _():
        ubuf[0:POOL_HALO, :] = jnp.zeros((POOL_HALO, POOL_W), _F32)

    @pl.when(tile != 0)
    def _():
        ubuf[0:POOL_HALO, :] = ubuf[TM:TM + POOL_HALO, :]

    ubuf[POOL_HALO:POOL_HALO + TM, :] = _dot(hn, w_ref[:, 0:POOL_W])
    q_ref[...] = (_dot(hn, w_ref[:, POOL_W:POOL_W + ATT_W]) * (1.0 / math.sqrt(HEAD_DIM))).astype(_BF16)
    kt_ref[...] = _dot_t(wkt_ref[...], hn).astype(_BF16)
    v_ref[...] = _dot(hn, w_ref[:, POOL_W + 2 * ATT_W:POOL_W + 3 * ATT_W]).astype(_BF16)

    pos = tile * TM + lax.broadcasted_iota(jnp.int32, (TM, 1), 0)
    for gi, win in enumerate(POOL_WINDOWS):
        cols = slice(gi * POOL_GROUP, (gi + 1) * POOL_GROUP)
        u = ubuf[POOL_HALO:POOL_HALO + TM, cols]
        s = u
        for d in range(1, win):
            s = s + ubuf[POOL_HALO - d:POOL_HALO - d + TM, cols]
        cnt = jnp.minimum(pos + 1, win).astype(_F32)
        y = (s / cnt - u).astype(_BF16)
        r = _dot(y, pw_ref[gi]) + pb_ref[:, cols]
        a_ref[:, cols] = (r * ps_ref[:, cols]).astype(_BF16)


def _mix_in(h, g, w_in, w_kt, pool_w, pool_b, pool_scale, tiles_per_seq):
    t, d = h.shape
    out = jax.ShapeDtypeStruct((t, ATT_W), _BF16)
    return pl.pallas_call(
        functools.partial(_mix_in_kernel, tiles_per_seq),
        grid=(t // TM,),
        in_specs=[
            _rows(d),
            _resident(g.shape),
            _resident(w_in.shape),
            _resident(w_kt.shape),
            _resident(pool_w.shape),
            _resident(pool_b.shape),
            _resident(pool_scale.shape),
        ],
        out_specs=[_rows(POOL_W), _rows(ATT_W), pl.BlockSpec((ATT_W, TM), lambda j: (0, j)), _rows(ATT_W)],
        out_shape=[jax.ShapeDtypeStruct((t, POOL_W), _BF16), out,
                   jax.ShapeDtypeStruct((ATT_W, t), _BF16), out],
        scratch_shapes=[pltpu.VMEM((POOL_HALO + TM, POOL_W), _F32)],
        compiler_params=pltpu.CompilerParams(
            dimension_semantics=("arbitrary",), vmem_limit_bytes=VMEM_LIMIT),
        name="mix_in",
    )(h, g, w_in, w_kt, pool_w, pool_b, pool_scale)


def _build_bias_tables(rb_ref, tbl):
    row = lax.broadcasted_iota(jnp.int32, (QSUB, KWIN), 0)
    col = lax.broadcasted_iota(jnp.int32, (QSUB, KWIN), 1)
    first_key = row & ~(CHUNK - 1)
    visible = (col >= first_key) & (col < first_key + (LEFT_CHUNKS + 1) * CHUNK)
    brow = lax.broadcasted_iota(jnp.int32, (QSUB, BAND), 0)
    bcol = lax.broadcasted_iota(jnp.int32, (QSUB, BAND), 1)
    for head in range(N_HEADS):
        r = rb_ref[head]
        far = r[:, 0:1]
        rolled = pltpu.roll(jnp.broadcast_to(r, (QSUB, BAND)), 0, 1, stride=1, stride_axis=0)
        band = jnp.where(bcol >= brow, rolled, far)
        full = jnp.concatenate([jnp.broadcast_to(far, (QSUB, KWIN - BAND)), band], axis=1)
        tbl[head // 2, (head % 2) * QSUB:(head % 2 + 1) * QSUB, :] = jnp.where(visible, full, NEG)
    tbl[NEG_TABLE] = jnp.full((2 * QSUB, KWIN), NEG, _F32)


def _attn_kernel(tiles_per_seq, q_ref, ktp_ref, ktc_ref, vp_ref, vc_ref, rb_ref, o_ref, tbl, s_s):
    @pl.when(pl.program_id(0) == 0)
    def _():
        _build_bias_tables(rb_ref, tbl)

    has_prev = (pl.program_id(0) % tiles_per_seq) != 0
    low_head = lax.broadcasted_iota(jnp.int32, (1, LANES), 1) < HEAD_DIM
    n_blocks = N_PAIRS * N_SUBS

    def geometry(n):
        pair, sub = divmod(n, N_SUBS)
        n_prev = TM - sub * QSUB
        return pair, slice(sub * QSUB, (sub + 1) * QSUB), slice(pair * LANES, (pair + 1) * LANES), n_prev

    def scores(n):
        pair, rows, cols, n_prev = geometry(n)
        q2 = q_ref[rows, cols]
        zero = jnp.zeros_like(q2)
        qq = jnp.concatenate([jnp.where(low_head, q2, zero), jnp.where(low_head, zero, q2)], axis=0)
        prev_tbl = jnp.where(has_prev, pair, NEG_TABLE)
        s_s[n % 2, :, 0:n_prev] = _dot(qq, ktp_ref[cols, TM - n_prev:TM]) + tbl[prev_tbl, :, 0:n_prev]
        s_s[n % 2, :, n_prev:KWIN] = _dot(qq, ktc_ref[cols, 0:KWIN - n_prev]) + tbl[pair, :, n_prev:KWIN]

    def outputs(n):
        _, rows, cols, n_prev = geometry(n)
        s = s_s[n % 2]
        e = jnp.exp(s - jnp.max(s, axis=-1, keepdims=True))
        p = (e * (1.0 / jnp.sum(e, axis=-1, keepdims=True))).astype(_BF16)
        o = (_dot(p[:, 0:n_prev], vp_ref[TM - n_prev:TM, cols])
             + _dot(p[:, n_prev:KWIN], vc_ref[0:KWIN - n_prev, cols]))
        o_ref[rows, cols] = jnp.where(low_head, o[0:QSUB], o[QSUB:2 * QSUB]).astype(_BF16)

    scores(0)
    for n in range(n_blocks):
        if n + 1 < n_blocks:
            scores(n + 1)
        outputs(n)


def _attn(q, kt, v, rb, tiles_per_seq):
    t, _ = q.shape

    def prev(j):
        return jnp.where(j % tiles_per_seq == 0, j, j - 1)

    return pl.pallas_call(
        functools.partial(_attn_kernel, tiles_per_seq),
        grid=(t // TM,),
        in_specs=[
            _rows(ATT_W),
            pl.BlockSpec((ATT_W, TM), lambda j: (0, prev(j))),
            pl.BlockSpec((ATT_W, TM), lambda j: (0, j)),
            pl.BlockSpec((TM, ATT_W), lambda j: (prev(j), 0)),
            _rows(ATT_W),
            _resident(rb.shape),
        ],
        out_specs=_rows(ATT_W),
        out_shape=jax.ShapeDtypeStruct((t, ATT_W), _BF16),
        scratch_shapes=[
            pltpu.VMEM((N_PAIRS + 1, 2 * QSUB, KWIN), _F32),
            pltpu.VMEM((2, 2 * QSUB, KWIN), _F32),
        ],
        compiler_params=pltpu.CompilerParams(
            dimension_semantics=("arbitrary",), vmem_limit_bytes=VMEM_LIMIT),
        name="attn",
    )(q, kt, kt, v, v, rb)


def _ff_chunks(d_ff):
    return tuple((off, min(FF_CHUNK, d_ff - off)) for off in range(0, d_ff, FF_CHUNK))


def _ffn_kernel(tiles_per_seq, d_ff, final, h_ref, a_ref, o_ref, p_ref, wout_ref, gf_ref, wup_ref, cw_ref,
                cb_ref, wdn_ref, gp_ref, wpg_ref, bpg_ref, wple_ref, fg_ref, out_ref,
                h1_s, hn_s, acc_s, gate_s, val_s, halo_s):
    @pl.when(pl.program_id(0) == 0)
    def _():
        halo_s[...] = jnp.zeros_like(halo_s)

    first = (pl.program_id(0) % tiles_per_seq) == 0
    h1 = (h_ref[...] + _dot(a_ref[...], wout_ref[0:POOL_W, :])
          + _dot(o_ref[...], wout_ref[POOL_W:POOL_W + ATT_W, :]))
    h1_s[...] = h1
    hn_s[...] = _rmsnorm(h1, gf_ref[...]).astype(_BF16)
    chunks = _ff_chunks(d_ff)

    def up(c):
        off, width = chunks[c]
        hn = hn_s[...]
        gate_s[c % 2, CONV_HALO:CONV_HALO + TM, 0:width] = _dot(hn, wup_ref[:, off:off + width])
        val_s[c % 2, :, 0:width] = _dot(hn, wup_ref[:, d_ff + off:d_ff + off + width])

    def down(c):
        off, width = chunks[c]
        gate = gate_s.at[c % 2]
        gate[0:CONV_HALO, 0:width] = jnp.where(first, 0.0, halo_s[:, off:off + width])
        halo_s[:, off:off + width] = gate[TM:TM + CONV_HALO, 0:width]
        y = gate[CONV_HALO:CONV_HALO + TM, 0:width] * cw_ref[CONV_K - 1:CONV_K, off:off + width]
        y = y + cb_ref[:, off:off + width]
        for kk in range(CONV_K - 1):
            lag = CONV_K - 1 - kk
            y = y + gate[CONV_HALO - lag:CONV_HALO - lag + TM, 0:width] * cw_ref[kk:kk + 1, off:off + width]
        act = 0.5 * y * (1.0 + lax.erf(y * (1.0 / math.sqrt(2.0))))
        part = _dot((act * val_s[c % 2, :, 0:width]).astype(_BF16), wdn_ref[off:off + width, :])
        if c == 0:
            acc_s[...] = part
        else:
            acc_s[...] += part

    up(0)
    for c in range(len(chunks)):
        if c + 1 < len(chunks):
            up(c + 1)
        down(c)

    h2 = h1_s[...] + acc_s[...]
    hn3 = _rmsnorm(h2, gp_ref[...]).astype(_BF16)
    gate = jax.nn.sigmoid(_dot(hn3, wpg_ref[...]) + bpg_ref[...])
    h3 = h2 + gate * _dot(p_ref[...].astype(_BF16), wple_ref[...])
    out_ref[...] = _rmsnorm(h3, fg_ref[...]) if final else h3


def _ffn(h, a, o, p, w_out, g_ffn, w_up, conv_w, conv_b, w_down, g_ple, w_pg, b_pg, w_ple, final_g,
         tiles_per_seq, final):
    t, d = h.shape
    d_ff = w_down.shape[0]
    residents = (w_out, g_ffn, w_up, conv_w, conv_b, w_down, g_ple, w_pg, b_pg, w_ple, final_g)
    return pl.pallas_call(
        functools.partial(_ffn_kernel, tiles_per_seq, d_ff, final),
        grid=(t // TM,),
        in_specs=[_rows(d), _rows(POOL_W), _rows(ATT_W), _rows(p.shape[1])]
        + [_resident(w.shape) for w in residents],
        out_specs=_rows(d),
        out_shape=jax.ShapeDtypeStruct((t, d), _F32),
        scratch_shapes=[
            pltpu.VMEM((TM, d), _F32),
            pltpu.VMEM((TM, d), _BF16),
            pltpu.VMEM((TM, d), _F32),
            pltpu.VMEM((2, CONV_HALO + TM, FF_CHUNK), _F32),
            pltpu.VMEM((2, TM, FF_CHUNK), _F32),
            pltpu.VMEM((CONV_HALO, d_ff), _F32),
        ],
        compiler_params=pltpu.CompilerParams(
            dimension_semantics=("arbitrary",), vmem_limit_bytes=VMEM_LIMIT),
        name="ffn",
    )(h, a, o, p, *residents)


def kernel(x, p, norm_mix_g, w_in, pool_w, pool_b, pool_scale, rel_bias, w_out, norm_ffn_g, w_up,
           conv_w, conv_b, w_down, norm_ple_g, w_ple_gate, b_ple_gate, w_ple, final_g):
    b, s, d = x.shape
    depth = w_in.shape[0]
    d_ff = w_down.shape[1]
    assert s % TM == 0 and d_ff % LANES == 0 and w_up.shape[2] == 2 * d_ff
    assert w_in.shape[2] == POOL_W + 3 * ATT_W and w_out.shape[1] == POOL_W + ATT_W
    assert rel_bias.shape[1:] == (N_HEADS, 2 * MAX_REL + 1)
    t = b * s
    tiles_per_seq = s // TM

    w_in_b = w_in.astype(_BF16)
    w_kt_b = w_in_b[:, :, POOL_W + ATT_W:POOL_W + 2 * ATT_W].transpose(0, 2, 1)
    pool_w_b = pool_w.astype(_BF16)
    w_out_b = w_out.astype(_BF16)
    w_up_b = w_up.astype(_BF16)
    w_down_b = w_down.astype(_BF16)
    w_pg_b = w_ple_gate.astype(_BF16)
    w_ple_b = w_ple.astype(_BF16)
    rb = rel_bias.astype(_F32)[:, :, :0:-1].reshape(depth, N_HEADS, 1, BAND)
    row = lambda v: v.reshape(1, -1)

    h = x.reshape(t, d)
    for i in range(depth):
        a, q, kt, v = _mix_in(h, row(norm_mix_g[i]), w_in_b[i], w_kt_b[i], pool_w_b[i], row(pool_b[i]),
                              row(pool_scale[i]), tiles_per_seq)
        o = _attn(q, kt, v, rb[i], tiles_per_seq)
        h = _ffn(h, a, o, p[i].reshape(t, -1), w_out_b[i], row(norm_ffn_g[i]), w_up_b[i], conv_w[i],
                 row(conv_b[i]), w_down_b[i], row(norm_ple_g[i]), w_pg_b[i], row(b_ple_gate[i]),
                 w_ple_b[i], row(final_g), tiles_per_seq, final=(i == depth - 1))
    return h.reshape(b, s, d)
```

```python
import functools
import math

import jax
import jax.numpy as jnp
from jax import lax
from jax.experimental import pallas as pl
from jax.experimental.pallas import tpu as pltpu

CHUNK = 64
POOL_WINDOWS = (2, 4, 8, 16)
POOL_GROUP = 128
POOL_W = POOL_GROUP * len(POOL_WINDOWS)
N_HEADS = 8
HEAD_DIM = 64
ATT_W = N_HEADS * HEAD_DIM
LEFT_CHUNKS = 8
LEFT = LEFT_CHUNKS * CHUNK
MAX_REL = 128
CONV_K = 3
EPS = 1e-6

LANES = 128
SUBLANES = 8
TM = LEFT
QSUB = 2 * CHUNK
KWIN = QSUB + LEFT
N_PAIRS = N_HEADS * HEAD_DIM // LANES
N_SUBS = TM // QSUB
BAND = 2 * MAX_REL
NEG_TABLE = N_PAIRS
POOL_HALO = 16
CONV_HALO = SUBLANES
FF_CHUNK = 768
NEG = -1e30
VMEM_LIMIT = 56 * 1024 * 1024

_BF16 = jnp.bfloat16
_F32 = jnp.float32


def _dot(a, b):
    return jnp.dot(a, b, preferred_element_type=_F32)


def _dot_t(a, b):
    return lax.dot_general(a, b, (((1,), (1,)), ((), ())), preferred_element_type=_F32)


def _rmsnorm(x, g):
    ms = jnp.mean(x * x, axis=-1, keepdims=True)
    return x * lax.rsqrt(ms + EPS) * g


def _resident(shape):
    return pl.BlockSpec(shape, lambda j: (0,) * len(shape), pipeline_mode=pl.Buffered(1))


def _rows(width):
    return pl.BlockSpec((TM, width), lambda j: (j, 0))


def _mix_in_kernel(tiles_per_seq, h_ref, g_ref, w_ref, wkt_ref, pw_ref, pb_ref, ps_ref,
                   a_ref, q_ref, kt_ref, v_ref, ubuf):
    tile = pl.program_id(0) % tiles_per_seq
    hn = _rmsnorm(h_ref[...], g_ref[...]).astype(_BF16)

    @pl.when(tile == 0)
    def _():
        ubuf[0:POOL_HALO, :] = jnp.zeros((POOL_HALO, POOL_W), _F32)

    @pl.when(tile != 0)
    def _():
        ubuf[0:POOL_HALO, :] = ubuf[TM:TM + POOL_HALO, :]

    ubuf[POOL_HALO:POOL_HALO + TM, :] = _dot(hn, w_ref[:, 0:POOL_W])
    q_ref[...] = (_dot(hn, w_ref[:, POOL_W:POOL_W + ATT_W]) * (1.0 / math.sqrt(HEAD_DIM))).astype(_BF16)
    kt_ref[...] = _dot_t(wkt_ref[...], hn).astype(_BF16)
    v = _dot(hn, w_ref[:, POOL_W + 2 * ATT_W:POOL_W + 3 * ATT_W]).astype(_BF16)
    for pair in range(N_PAIRS):
        v_ref[:, 2 * pair * LANES:(2 * pair + 1) * LANES] = v[:, pair * LANES:(pair + 1) * LANES]
        v_ref[:, (2 * pair + 1) * LANES:(2 * pair + 2) * LANES] = jnp.ones((TM, LANES), _BF16)

    pos = tile * TM + lax.broadcasted_iota(jnp.int32, (TM, 1), 0)
    for gi, win in enumerate(POOL_WINDOWS):
        cols = slice(gi * POOL_GROUP, (gi + 1) * POOL_GROUP)
        u = ubuf[POOL_HALO:POOL_HALO + TM, cols]
        s = u
        for d in range(1, win):
            s = s + ubuf[POOL_HALO - d:POOL_HALO - d + TM, cols]
        cnt = jnp.minimum(pos + 1, win).astype(_F32)
        y = (s / cnt - u).astype(_BF16)
        r = _dot(y, pw_ref[gi]) + pb_ref[:, cols]
        a_ref[:, cols] = (r * ps_ref[:, cols]).astype(_BF16)


def _mix_in(h, g, w_in, w_kt, pool_w, pool_b, pool_scale, tiles_per_seq):
    t, d = h.shape
    out = jax.ShapeDtypeStruct((t, ATT_W), _BF16)
    return pl.pallas_call(
        functools.partial(_mix_in_kernel, tiles_per_seq),
        grid=(t // TM,),
        in_specs=[
            _rows(d),
            _resident(g.shape),
            _resident(w_in.shape),
            _resident(w_kt.shape),
            _resident(pool_w.shape),
            _resident(pool_b.shape),
            _resident(pool_scale.shape),
        ],
        out_specs=[_rows(POOL_W), _rows(ATT_W), pl.BlockSpec((ATT_W, TM), lambda j: (0, j)), _rows(2 * ATT_W)],
        out_shape=[jax.ShapeDtypeStruct((t, POOL_W), _BF16), out,
                   jax.ShapeDtypeStruct((ATT_W, t), _BF16), jax.ShapeDtypeStruct((t, 2 * ATT_W), _BF16)],
        scratch_shapes=[pltpu.VMEM((POOL_HALO + TM, POOL_W), _F32)],
        compiler_params=pltpu.CompilerParams(
            dimension_semantics=("arbitrary",), vmem_limit_bytes=VMEM_LIMIT),
        name="mix_in",
    )(h, g, w_in, w_kt, pool_w, pool_b, pool_scale)


def _build_bias_tables(rb_ref, tbl):
    row = lax.broadcasted_iota(jnp.int32, (QSUB, KWIN), 0)
    col = lax.broadcasted_iota(jnp.int32, (QSUB, KWIN), 1)
    first_key = row & ~(CHUNK - 1)
    visible = (col >= first_key) & (col < first_key + (LEFT_CHUNKS + 1) * CHUNK)
    brow = lax.broadcasted_iota(jnp.int32, (QSUB, BAND), 0)
    bcol = lax.broadcasted_iota(jnp.int32, (QSUB, BAND), 1)
    for head in range(N_HEADS):
        r = rb_ref[head]
        far = r[:, 0:1]
        rolled = pltpu.roll(jnp.broadcast_to(r, (QSUB, BAND)), 0, 1, stride=1, stride_axis=0)
        band = jnp.where(bcol >= brow, rolled, far)
        full = jnp.concatenate([jnp.broadcast_to(far, (QSUB, KWIN - BAND)), band], axis=1)
        tbl[head // 2, (head % 2) * QSUB:(head % 2 + 1) * QSUB, :] = jnp.where(visible, full, NEG)
    tbl[NEG_TABLE] = jnp.full((2 * QSUB, KWIN), NEG, _F32)


def _attn_kernel(tiles_per_seq, q_ref, ktp_ref, ktc_ref, vp_ref, vc_ref, rb_ref, o_ref, tbl, s_s):
    @pl.when(pl.program_id(0) == 0)
    def _():
        _build_bias_tables(rb_ref, tbl)

    has_prev = (pl.program_id(0) % tiles_per_seq) != 0
    low_head = lax.broadcasted_iota(jnp.int32, (1, LANES), 1) < HEAD_DIM
    n_blocks = N_PAIRS * N_SUBS

    def geometry(n):
        pair, sub = divmod(n, N_SUBS)
        n_prev = TM - sub * QSUB
        return pair, slice(sub * QSUB, (sub + 1) * QSUB), slice(pair * LANES, (pair + 1) * LANES), n_prev

    def scores(n):
        pair, rows, cols, n_prev = geometry(n)
        q2 = q_ref[rows, cols]
        zero = jnp.zeros_like(q2)
        qq = jnp.concatenate([jnp.where(low_head, q2, zero), jnp.where(low_head, zero, q2)], axis=0)
        prev_tbl = jnp.where(has_prev, pair, NEG_TABLE)
        s_s[n % 2, :, 0:n_prev] = _dot(qq, ktp_ref[cols, TM - n_prev:TM]) + tbl[prev_tbl, :, 0:n_prev]
        s_s[n % 2, :, n_prev:KWIN] = _dot(qq, ktc_ref[cols, 0:KWIN - n_prev]) + tbl[pair, :, n_prev:KWIN]

    def outputs(n):
        pair, rows, cols, n_prev = geometry(n)
        vcols = slice(2 * pair * LANES, (2 * pair + 2) * LANES)
        m = jnp.max(s_s[n % 2], axis=-1, keepdims=True)
        e_prev = jnp.exp(s_s[n % 2, :, 0:n_prev] - m).astype(_BF16)
        e_cur = jnp.exp(s_s[n % 2, :, n_prev:KWIN] - m).astype(_BF16)
        ov = _dot(e_prev, vp_ref[TM - n_prev:TM, vcols]) + _dot(e_cur, vc_ref[0:KWIN - n_prev, vcols])
        o = ov[:, 0:LANES] / ov[:, LANES:2 * LANES]
        o_ref[rows, cols] = jnp.where(low_head, o[0:QSUB], o[QSUB:2 * QSUB]).astype(_BF16)

    scores(0)
    for n in range(n_blocks):
        if n + 1 < n_blocks:
            scores(n + 1)
        outputs(n)


def _attn(q, kt, v, rb, tiles_per_seq):
    t, _ = q.shape

    def prev(j):
        return jnp.where(j % tiles_per_seq == 0, j, j - 1)

    return pl.pallas_call(
        functools.partial(_attn_kernel, tiles_per_seq),
        grid=(t // TM,),
        in_specs=[
            _rows(ATT_W),
            pl.BlockSpec((ATT_W, TM), lambda j: (0, prev(j))),
            pl.BlockSpec((ATT_W, TM), lambda j: (0, j)),
            pl.BlockSpec((TM, 2 * ATT_W), lambda j: (prev(j), 0)),
            _rows(2 * ATT_W),
            _resident(rb.shape),
        ],
        out_specs=_rows(ATT_W),
        out_shape=jax.ShapeDtypeStruct((t, ATT_W), _BF16),
        scratch_shapes=[
            pltpu.VMEM((N_PAIRS + 1, 2 * QSUB, KWIN), _F32),
            pltpu.VMEM((2, 2 * QSUB, KWIN), _F32),
        ],
        compiler_params=pltpu.CompilerParams(
            dimension_semantics=("arbitrary",), vmem_limit_bytes=VMEM_LIMIT),
        name="attn",
    )(q, kt, kt, v, v, rb)


def _ff_chunks(d_ff):
    return tuple((off, min(FF_CHUNK, d_ff - off)) for off in range(0, d_ff, FF_CHUNK))


def _ffn_kernel(tiles_per_seq, d_ff, final, h_ref, a_ref, o_ref, p_ref, wout_ref, gf_ref, wup_ref, cw_ref,
                cb_ref, wdn_ref, gp_ref, wpg_ref, bpg_ref, wple_ref, fg_ref, out_ref,
                h1_s, hn_s, acc_s, gate_s, val_s, halo_s):
    @pl.when(pl.program_id(0) == 0)
    def _():
        halo_s[...] = jnp.zeros_like(halo_s)

    first = (pl.program_id(0) % tiles_per_seq) == 0
    h1 = (h_ref[...] + _dot(a_ref[...], wout_ref[0:POOL_W, :])
          + _dot(o_ref[...], wout_ref[POOL_W:POOL_W + ATT_W, :]))
    h1_s[...] = h1
    hn_s[...] = _rmsnorm(h1, gf_ref[...]).astype(_BF16)
    chunks = _ff_chunks(d_ff)

    def up(c):
        off, width = chunks[c]
        hn = hn_s[...]
        gate_s[c % 2, CONV_HALO:CONV_HALO + TM, 0:width] = _dot(hn, wup_ref[:, off:off + width])
        val_s[c % 2, :, 0:width] = _dot(hn, wup_ref[:, d_ff + off:d_ff + off + width])

    def down(c):
        off, width = chunks[c]
        gate = gate_s.at[c % 2]
        gate[0:CONV_HALO, 0:width] = jnp.where(first, 0.0, halo_s[:, off:off + width])
        halo_s[:, off:off + width] = gate[TM:TM + CONV_HALO, 0:width]
        y = gate[CONV_HALO:CONV_HALO + TM, 0:width] * cw_ref[CONV_K - 1:CONV_K, off:off + width]
        y = y + cb_ref[:, off:off + width]
        for kk in range(CONV_K - 1):
            lag = CONV_K - 1 - kk
            y = y + gate[CONV_HALO - lag:CONV_HALO - lag + TM, 0:width] * cw_ref[kk:kk + 1, off:off + width]
        act = 0.5 * y * (1.0 + lax.erf(y * (1.0 / math.sqrt(2.0))))
        part = _dot((act * val_s[c % 2, :, 0:width]).astype(_BF16), wdn_ref[off:off + width, :])
        if c == 0:
            acc_s[...] = part
        else:
            acc_s[...] += part

    up(0)
    for c in range(len(chunks)):
        if c + 1 < len(chunks):
            up(c + 1)
        down(c)

    h2 = h1_s[...] + acc_s[...]
    hn3 = _rmsnorm(h2, gp_ref[...]).astype(_BF16)
    gate = jax.nn.sigmoid(_dot(hn3, wpg_ref[...]) + bpg_ref[...])
    h3 = h2 + gate * _dot(p_ref[...].astype(_BF16), wple_ref[...])
    out_ref[...] = _rmsnorm(h3, fg_ref[...]) if final else h3


def _ffn(h, a, o, p, w_out, g_ffn, w_up, conv_w, conv_b, w_down, g_ple, w_pg, b_pg, w_ple, final_g,
         tiles_per_seq, final):
    t, d = h.shape
    d_ff = w_down.shape[0]
    residents = (w_out, g_ffn, w_up, conv_w, conv_b, w_down, g_ple, w_pg, b_pg, w_ple, final_g)
    return pl.pallas_call(
        functools.partial(_ffn_kernel, tiles_per_seq, d_ff, final),
        grid=(t // TM,),
        in_specs=[_rows(d), _rows(POOL_W), _rows(ATT_W), _rows(p.shape[1])]
        + [_resident(w.shape) for w in residents],
        out_specs=_rows(d),
        out_shape=jax.ShapeDtypeStruct((t, d), _F32),
        scratch_shapes=[
            pltpu.VMEM((TM, d), _F32),
            pltpu.VMEM((TM, d), _BF16),
            pltpu.VMEM((TM, d), _F32),
            pltpu.VMEM((2, CONV_HALO + TM, FF_CHUNK), _F32),
            pltpu.VMEM((2, TM, FF_CHUNK), _F32),
            pltpu.VMEM((CONV_HALO, d_ff), _F32),
        ],
        compiler_params=pltpu.CompilerParams(
            dimension_semantics=("arbitrary",), vmem_limit_bytes=VMEM_LIMIT),
        name="ffn",
    )(h, a, o, p, *residents)


def kernel(x, p, norm_mix_g, w_in, pool_w, pool_b, pool_scale, rel_bias, w_out, norm_ffn_g, w_up,
           conv_w, conv_b, w_down, norm_ple_g, w_ple_gate, b_ple_gate, w_ple, final_g):
    b, s, d = x.shape
    depth = w_in.shape[0]
    d_ff = w_down.shape[1]
    assert s % TM == 0 and d_ff % LANES == 0 and w_up.shape[2] == 2 * d_ff
    assert w_in.shape[2] == POOL_W + 3 * ATT_W and w_out.shape[1] == POOL_W + ATT_W
    assert rel_bias.shape[1:] == (N_HEADS, 2 * MAX_REL + 1)
    t = b * s
    tiles_per_seq = s // TM

    w_in_b = w_in.astype(_BF16)
    w_kt_b = w_in_b[:, :, POOL_W + ATT_W:POOL_W + 2 * ATT_W].transpose(0, 2, 1)
    pool_w_b = pool_w.astype(_BF16)
    w_out_b = w_out.astype(_BF16)
    w_up_b = w_up.astype(_BF16)
    w_down_b = w_down.astype(_BF16)
    w_pg_b = w_ple_gate.astype(_BF16)
    w_ple_b = w_ple.astype(_BF16)
    rb = rel_bias.astype(_F32)[:, :, :0:-1].reshape(depth, N_HEADS, 1, BAND)
    row = lambda v: v.reshape(1, -1)

    h = x.reshape(t, d)
    for i in range(depth):
        a, q, kt, v = _mix_in(h, row(norm_mix_g[i]), w_in_b[i], w_kt_b[i], pool_w_b[i], row(pool_b[i]),
                              row(pool_scale[i]), tiles_per_seq)
        o = _attn(q, kt, v, rb[i], tiles_per_seq)
        h = _ffn(h, a, o, p[i].reshape(t, -1), w_out_b[i], row(norm_ffn_g[i]), w_up_b[i], conv_w[i],
                 row(conv_b[i]), w_down_b[i], row(norm_ple_g[i]), w_pg_b[i], row(b_ple_gate[i]),
                 w_ple_b[i], row(final_g), tiles_per_seq, final=(i == depth - 1))
    return h.reshape(b, s, d)
```

```python
import functools
import math

import jax
import jax.numpy as jnp
from jax import lax
from jax.experimental import pallas as pl
from jax.experimental.pallas import tpu as pltpu

CHUNK = 64
POOL_WINDOWS = (2, 4, 8, 16)
POOL_GROUP = 128
POOL_W = POOL_GROUP * len(POOL_WINDOWS)
N_HEADS = 8
HEAD_DIM = 64
ATT_W = N_HEADS * HEAD_DIM
LEFT_CHUNKS = 8
LEFT = LEFT_CHUNKS * CHUNK
MAX_REL = 128
CONV_K = 3
EPS = 1e-6

LANES = 128
SUBLANES = 8
TM = LEFT
QSUB = 2 * CHUNK
KWIN = QSUB + LEFT
N_PAIRS = N_HEADS * HEAD_DIM // LANES
N_SUBS = TM // QSUB
BAND = 2 * MAX_REL
NEG_TABLE = N_PAIRS
POOL_HALO = 16
CONV_HALO = SUBLANES
FF_CHUNK = 768
NEG = -1e30
VMEM_LIMIT = 56 * 1024 * 1024

_BF16 = jnp.bfloat16
_F32 = jnp.float32


def _dot(a, b):
    return jnp.dot(a, b, preferred_element_type=_F32)


def _dot_t(a, b):
    return lax.dot_general(a, b, (((1,), (1,)), ((), ())), preferred_element_type=_F32)


def _rmsnorm(x, g):
    ms = jnp.mean(x * x, axis=-1, keepdims=True)
    return x * lax.rsqrt(ms + EPS) * g


def _resident(shape, layer=None):
    if layer is None:
        return pl.BlockSpec(shape, lambda j: (0,) * len(shape), pipeline_mode=pl.Buffered(1))
    return pl.BlockSpec((None,) + tuple(shape[1:]), lambda j: (layer,) + (0,) * (len(shape) - 1),
                        pipeline_mode=pl.Buffered(1))


def _rows(width):
    return pl.BlockSpec((TM, width), lambda j: (j, 0))


def _mix_in_kernel(tiles_per_seq, h_ref, g_ref, w_ref, wkt_ref, pw_ref, pb_ref, ps_ref,
                   a_ref, q_ref, kt_ref, v_ref, ubuf):
    tile = pl.program_id(0) % tiles_per_seq
    hn = _rmsnorm(h_ref[...], g_ref[...]).astype(_BF16)

    @pl.when(tile == 0)
    def _():
        ubuf[0:POOL_HALO, :] = jnp.zeros((POOL_HALO, POOL_W), _F32)

    @pl.when(tile != 0)
    def _():
        ubuf[0:POOL_HALO, :] = ubuf[TM:TM + POOL_HALO, :]

    ubuf[POOL_HALO:POOL_HALO + TM, :] = _dot(hn, w_ref[:, 0:POOL_W])
    q_ref[...] = (_dot(hn, w_ref[:, POOL_W:POOL_W + ATT_W]) * (1.0 / math.sqrt(HEAD_DIM))).astype(_BF16)
    kt_ref[...] = _dot_t(wkt_ref[...], hn).astype(_BF16)
    v = _dot(hn, w_ref[:, POOL_W + 2 * ATT_W:POOL_W + 3 * ATT_W]).astype(_BF16)
    for pair in range(N_PAIRS):
        v_ref[:, 2 * pair * LANES:(2 * pair + 1) * LANES] = v[:, pair * LANES:(pair + 1) * LANES]
        v_ref[:, (2 * pair + 1) * LANES:(2 * pair + 2) * LANES] = jnp.ones((TM, LANES), _BF16)

    pos = tile * TM + lax.broadcasted_iota(jnp.int32, (TM, 1), 0)
    for gi, win in enumerate(POOL_WINDOWS):
        cols = slice(gi * POOL_GROUP, (gi + 1) * POOL_GROUP)
        u = ubuf[POOL_HALO:POOL_HALO + TM, cols]
        s = u
        for d in range(1, win):
            s = s + ubuf[POOL_HALO - d:POOL_HALO - d + TM, cols]
        cnt = jnp.minimum(pos + 1, win).astype(_F32)
        y = (s / cnt - u).astype(_BF16)
        r = _dot(y, pw_ref[gi]) + pb_ref[:, cols]
        a_ref[:, cols] = (r * ps_ref[:, cols]).astype(_BF16)


def _mix_in(h, layer, g, w_in, w_kt, pool_w, pool_b, pool_scale, tiles_per_seq):
    t, d = h.shape
    out = jax.ShapeDtypeStruct((t, ATT_W), _BF16)
    return pl.pallas_call(
        functools.partial(_mix_in_kernel, tiles_per_seq),
        grid=(t // TM,),
        in_specs=[
            _rows(d),
            _resident(g.shape, layer),
            _resident(w_in.shape, layer),
            _resident(w_kt.shape, layer),
            _resident(pool_w.shape, layer),
            _resident(pool_b.shape, layer),
            _resident(pool_scale.shape, layer),
        ],
        out_specs=[_rows(POOL_W), _rows(ATT_W), pl.BlockSpec((ATT_W, TM), lambda j: (0, j)), _rows(2 * ATT_W)],
        out_shape=[jax.ShapeDtypeStruct((t, POOL_W), _BF16), out,
                   jax.ShapeDtypeStruct((ATT_W, t), _BF16), jax.ShapeDtypeStruct((t, 2 * ATT_W), _BF16)],
        scratch_shapes=[pltpu.VMEM((POOL_HALO + TM, POOL_W), _F32)],
        compiler_params=pltpu.CompilerParams(
            dimension_semantics=("arbitrary",), vmem_limit_bytes=VMEM_LIMIT),
        name="mix_in",
    )(h, g, w_in, w_kt, pool_w, pool_b, pool_scale)


def _build_bias_tables(rb_ref, tbl):
    row = lax.broadcasted_iota(jnp.int32, (QSUB, KWIN), 0)
    col = lax.broadcasted_iota(jnp.int32, (QSUB, KWIN), 1)
    first_key = row & ~(CHUNK - 1)
    visible = (col >= first_key) & (col < first_key + (LEFT_CHUNKS + 1) * CHUNK)
    brow = lax.broadcasted_iota(jnp.int32, (QSUB, BAND), 0)
    bcol = lax.broadcasted_iota(jnp.int32, (QSUB, BAND), 1)
    for head in range(N_HEADS):
        r = rb_ref[head]
        far = r[:, 0:1]
        rolled = pltpu.roll(jnp.broadcast_to(r, (QSUB, BAND)), 0, 1, stride=1, stride_axis=0)
        band = jnp.where(bcol >= brow, rolled, far)
        full = jnp.concatenate([jnp.broadcast_to(far, (QSUB, KWIN - BAND)), band], axis=1)
        tbl[head // 2, (head % 2) * QSUB:(head % 2 + 1) * QSUB, :] = jnp.where(visible, full, NEG)
    tbl[NEG_TABLE] = jnp.full((2 * QSUB, KWIN), NEG, _F32)


def _attn_kernel(tiles_per_seq, q_ref, ktp_ref, ktc_ref, vp_ref, vc_ref, rb_ref, o_ref, tbl, s_s):
    @pl.when(pl.program_id(0) == 0)
    def _():
        _build_bias_tables(rb_ref, tbl)

    has_prev = (pl.program_id(0) % tiles_per_seq) != 0
    low_head = lax.broadcasted_iota(jnp.int32, (1, LANES), 1) < HEAD_DIM
    n_blocks = N_PAIRS * N_SUBS

    def geometry(n):
        pair, sub = divmod(n, N_SUBS)
        n_prev = TM - sub * QSUB
        return pair, slice(sub * QSUB, (sub + 1) * QSUB), slice(pair * LANES, (pair + 1) * LANES), n_prev

    def scores(n):
        pair, rows, cols, n_prev = geometry(n)
        q2 = q_ref[rows, cols]
        zero = jnp.zeros_like(q2)
        qq = jnp.concatenate([jnp.where(low_head, q2, zero), jnp.where(low_head, zero, q2)], axis=0)
        prev_tbl = jnp.where(has_prev, pair, NEG_TABLE)
        s_s[n % 2, :, 0:n_prev] = _dot(qq, ktp_ref[cols, TM - n_prev:TM]) + tbl[prev_tbl, :, 0:n_prev]
        s_s[n % 2, :, n_prev:KWIN] = _dot(qq, ktc_ref[cols, 0:KWIN - n_prev]) + tbl[pair, :, n_prev:KWIN]

    def outputs(n):
        pair, rows, cols, n_prev = geometry(n)
        vcols = slice(2 * pair * LANES, (2 * pair + 2) * LANES)
        m = jnp.max(s_s[n % 2], axis=-1, keepdims=True)
        e_prev = jnp.exp(s_s[n % 2, :, 0:n_prev] - m).astype(_BF16)
        e_cur = jnp.exp(s_s[n % 2, :, n_prev:KWIN] - m).astype(_BF16)
        ov = _dot(e_prev, vp_ref[TM - n_prev:TM, vcols]) + _dot(e_cur, vc_ref[0:KWIN - n_prev, vcols])
        o = ov[:, 0:LANES] / ov[:, LANES:2 * LANES]
        o_ref[rows, cols] = jnp.where(low_head, o[0:QSUB], o[QSUB:2 * QSUB]).astype(_BF16)

    scores(0)
    for n in range(n_blocks):
        if n + 1 < n_blocks:
            scores(n + 1)
        outputs(n)


def _attn(q, kt, v, layer, rb, tiles_per_seq):
    t, _ = q.shape

    def prev(j):
        return jnp.where(j % tiles_per_seq == 0, j, j - 1)

    return pl.pallas_call(
        functools.partial(_attn_kernel, tiles_per_seq),
        grid=(t // TM,),
        in_specs=[
            _rows(ATT_W),
            pl.BlockSpec((ATT_W, TM), lambda j: (0, prev(j))),
            pl.BlockSpec((ATT_W, TM), lambda j: (0, j)),
            pl.BlockSpec((TM, 2 * ATT_W), lambda j: (prev(j), 0)),
            _rows(2 * ATT_W),
            _resident(rb.shape, layer),
        ],
        out_specs=_rows(ATT_W),
        out_shape=jax.ShapeDtypeStruct((t, ATT_W), _BF16),
        scratch_shapes=[
            pltpu.VMEM((N_PAIRS + 1, 2 * QSUB, KWIN), _F32),
            pltpu.VMEM((2, 2 * QSUB, KWIN), _F32),
        ],
        compiler_params=pltpu.CompilerParams(
            dimension_semantics=("arbitrary",), vmem_limit_bytes=VMEM_LIMIT),
        name="attn",
    )(q, kt, kt, v, v, rb)


def _ff_chunks(d_ff):
    return tuple((off, min(FF_CHUNK, d_ff - off)) for off in range(0, d_ff, FF_CHUNK))


def _ffn_kernel(tiles_per_seq, d_ff, final, h_ref, a_ref, o_ref, p_ref, wout_ref, gf_ref, wup_ref, cw_ref,
                cb_ref, wdn_ref, gp_ref, wpg_ref, bpg_ref, wple_ref, fg_ref, out_ref,
                h1_s, hn_s, acc_s, gate_s, val_s, halo_s):
    @pl.when(pl.program_id(0) == 0)
    def _():
        halo_s[...] = jnp.zeros_like(halo_s)

    first = (pl.program_id(0) % tiles_per_seq) == 0
    h1 = (h_ref[...] + _dot(a_ref[...], wout_ref[0:POOL_W, :])
          + _dot(o_ref[...], wout_ref[POOL_W:POOL_W + ATT_W, :]))
    h1_s[...] = h1
    hn_s[...] = _rmsnorm(h1, gf_ref[...]).astype(_BF16)
    chunks = _ff_chunks(d_ff)

    def up(c):
        off, width = chunks[c]
        hn = hn_s[...]
        gate_s[c % 2, CONV_HALO:CONV_HALO + TM, 0:width] = _dot(hn, wup_ref[:, off:off + width])
        val_s[c % 2, :, 0:width] = _dot(hn, wup_ref[:, d_ff + off:d_ff + off + width])

    def down(c):
        off, width = chunks[c]
        gate = gate_s.at[c % 2]
        gate[0:CONV_HALO, 0:width] = jnp.where(first, 0.0, halo_s[:, off:off + width])
        halo_s[:, off:off + width] = gate[TM:TM + CONV_HALO, 0:width]
        y = gate[CONV_HALO:CONV_HALO + TM, 0:width] * cw_ref[CONV_K - 1:CONV_K, off:off + width]
        y = y + cb_ref[:, off:off + width]
        for kk in range(CONV_K - 1):
            lag = CONV_K - 1 - kk
            y = y + gate[CONV_HALO - lag:CONV_HALO - lag + TM, 0:width] * cw_ref[kk:kk + 1, off:off + width]
        act = 0.5 * y * (1.0 + lax.erf(y * (1.0 / math.sqrt(2.0))))
        part = _dot((act * val_s[c % 2, :, 0:width]).astype(_BF16), wdn_ref[off:off + width, :])
        if c == 0:
            acc_s[...] = part
        else:
            acc_s[...] += part

    up(0)
    for c in range(len(chunks)):
        if c + 1 < len(chunks):
            up(c + 1)
        down(c)

    h2 = h1_s[...] + acc_s[...]
    hn3 = _rmsnorm(h2, gp_ref[...]).astype(_BF16)
    gate = jax.nn.sigmoid(_dot(hn3, wpg_ref[...]) + bpg_ref[...])
    h3 = h2 + gate * _dot(p_ref[...].astype(_BF16), wple_ref[...])
    out_ref[...] = _rmsnorm(h3, fg_ref[...]) if final else h3


def _ffn(h, a, o, layer, p, w_out, g_ffn, w_up, conv_w, conv_b, w_down, g_ple, w_pg, b_pg, w_ple, final_g,
         tiles_per_seq, final):
    t, d = h.shape
    d_ff = w_down.shape[1]
    per_layer = (w_out, g_ffn, w_up, conv_w, conv_b, w_down, g_ple, w_pg, b_pg, w_ple)
    return pl.pallas_call(
        functools.partial(_ffn_kernel, tiles_per_seq, d_ff, final),
        grid=(t // TM,),
        in_specs=[_rows(d), _rows(POOL_W), _rows(ATT_W),
                  pl.BlockSpec((None, TM, p.shape[2]), lambda j: (layer, j, 0))]
        + [_resident(w.shape, layer) for w in per_layer] + [_resident(final_g.shape)],
        out_specs=_rows(d),
        out_shape=jax.ShapeDtypeStruct((t, d), _F32),
        scratch_shapes=[
            pltpu.VMEM((TM, d), _F32),
            pltpu.VMEM((TM, d), _BF16),
            pltpu.VMEM((TM, d), _F32),
            pltpu.VMEM((2, CONV_HALO + TM, FF_CHUNK), _F32),
            pltpu.VMEM((2, TM, FF_CHUNK), _F32),
            pltpu.VMEM((CONV_HALO, d_ff), _F32),
        ],
        compiler_params=pltpu.CompilerParams(
            dimension_semantics=("arbitrary",), vmem_limit_bytes=VMEM_LIMIT),
        name="ffn",
    )(h, a, o, p, *per_layer, final_g)


def kernel(x, p, norm_mix_g, w_in, pool_w, pool_b, pool_scale, rel_bias, w_out, norm_ffn_g, w_up,
           conv_w, conv_b, w_down, norm_ple_g, w_ple_gate, b_ple_gate, w_ple, final_g):
    b, s, d = x.shape
    depth = w_in.shape[0]
    d_ff = w_down.shape[1]
    assert s % TM == 0 and d_ff % LANES == 0 and w_up.shape[2] == 2 * d_ff
    assert w_in.shape[2] == POOL_W + 3 * ATT_W and w_out.shape[1] == POOL_W + ATT_W
    assert rel_bias.shape[1:] == (N_HEADS, 2 * MAX_REL + 1)
    t = b * s
    tiles_per_seq = s // TM

    w_in_b = w_in.astype(_BF16)
    w_kt_b = w_in_b[:, :, POOL_W + ATT_W:POOL_W + 2 * ATT_W].transpose(0, 2, 1)
    pool_w_b = pool_w.astype(_BF16)
    w_out_b = w_out.astype(_BF16)
    w_up_b = w_up.astype(_BF16)
    w_down_b = w_down.astype(_BF16)
    w_pg_b = w_ple_gate.astype(_BF16)
    w_ple_b = w_ple.astype(_BF16)
    rb = rel_bias.astype(_F32)[:, :, :0:-1].reshape(depth, N_HEADS, 1, BAND)
    rows = lambda v: v.reshape(depth, 1, -1)
    p_rows = p.reshape(depth, t, -1)

    h = x.reshape(t, d)
    for i in range(depth):
        a, q, kt, v = _mix_in(h, i, rows(norm_mix_g), w_in_b, w_kt_b, pool_w_b, rows(pool_b), rows(pool_scale),
                              tiles_per_seq)
        o = _attn(q, kt, v, i, rb, tiles_per_seq)
        h = _ffn(h, a, o, i, p_rows, w_out_b, rows(norm_ffn_g), w_up_b, conv_w, rows(conv_b), w_down_b,
                 rows(norm_ple_g), w_pg_b, rows(b_ple_gate), w_ple_b, final_g.reshape(1, -1), tiles_per_seq,
                 final=(i == depth - 1))
    return h.reshape(b, s, d)
```

```python
import functools
import math

import jax
import jax.numpy as jnp
from jax import lax
from jax.experimental import pallas as pl
from jax.experimental.pallas import tpu as pltpu

CHUNK = 64
POOL_WINDOWS = (2, 4, 8, 16)
POOL_GROUP = 128
POOL_W = POOL_GROUP * len(POOL_WINDOWS)
N_HEADS = 8
HEAD_DIM = 64
ATT_W = N_HEADS * HEAD_DIM
LEFT_CHUNKS = 8
LEFT = LEFT_CHUNKS * CHUNK
MAX_REL = 128
CONV_K = 3
EPS = 1e-6

LANES = 128
SUBLANES = 8
TM = LEFT
QSUB = 2 * CHUNK
KWIN = QSUB + LEFT
N_PAIRS = N_HEADS * HEAD_DIM // LANES
N_SUBS = TM // QSUB
BAND = 2 * MAX_REL
NEG_TABLE = N_PAIRS
ROW_BLOCK = 256
POOL_HALO = 16
CONV_HALO = SUBLANES
FF_CHUNK = 768
NEG = -1e30
VMEM_LIMIT = 56 * 1024 * 1024

_BF16 = jnp.bfloat16
_F32 = jnp.float32


def _dot(a, b):
    return jnp.dot(a, b, preferred_element_type=_F32)


def _dot_t(a, b):
    return lax.dot_general(a, b, (((1,), (1,)), ((), ())), preferred_element_type=_F32)


def _rmsnorm(x, g):
    ms = jnp.mean(x * x, axis=-1, keepdims=True)
    return x * lax.rsqrt(ms + EPS) * g


def _resident(shape, layer=None):
    if layer is None:
        return pl.BlockSpec(shape, lambda j: (0,) * len(shape), pipeline_mode=pl.Buffered(1))
    return pl.BlockSpec((None,) + tuple(shape[1:]), lambda j: (layer,) + (0,) * (len(shape) - 1),
                        pipeline_mode=pl.Buffered(1))


def _rows(width):
    return pl.BlockSpec((TM, width), lambda j: (j, 0))


def _mix_in_kernel(tiles_per_seq, h_ref, g_ref, w_ref, wkt_ref, pw_ref, pb_ref, ps_ref,
                   a_ref, q_ref, kt_ref, v_ref, ubuf):
    @pl.when(pl.program_id(0) == 0)
    def _():
        ubuf[...] = jnp.zeros_like(ubuf)

    tile = pl.program_id(0) % tiles_per_seq
    ubuf[0:POOL_HALO, :] = jnp.where(tile == 0, 0.0, ubuf[TM:TM + POOL_HALO, :])

    row_blocks = [slice(r, r + ROW_BLOCK) for r in range(0, TM, ROW_BLOCK)]
    hn = [_rmsnorm(h_ref[rows, :], g_ref[...]).astype(_BF16) for rows in row_blocks]

    def pool(gi):
        win = POOL_WINDOWS[gi]
        cols = slice(gi * POOL_GROUP, (gi + 1) * POOL_GROUP)
        u = ubuf[POOL_HALO:POOL_HALO + TM, cols]
        s = u
        for d in range(1, win):
            s = s + ubuf[POOL_HALO - d:POOL_HALO - d + TM, cols]
        pos = tile * TM + lax.broadcasted_iota(jnp.int32, (TM, 1), 0)
        cnt = jnp.minimum(pos + 1, win).astype(_F32)
        y = (s / cnt - u).astype(_BF16)
        r = _dot(y, pw_ref[gi]) + pb_ref[:, cols]
        a_ref[:, cols] = (r * ps_ref[:, cols]).astype(_BF16)

    for rows, x in zip(row_blocks, hn):
        ubuf[POOL_HALO + rows.start:POOL_HALO + rows.stop, :] = _dot(x, w_ref[:, 0:POOL_W])
    pool(0)
    for rows, x in zip(row_blocks, hn):
        q = _dot(x, w_ref[:, POOL_W:POOL_W + ATT_W]) * (1.0 / math.sqrt(HEAD_DIM))
        q_ref[rows, :] = q.astype(_BF16)
    pool(1)
    for rows, x in zip(row_blocks, hn):
        kt_ref[:, rows] = _dot_t(wkt_ref[...], x).astype(_BF16)
    pool(2)
    pool(3)
    for rows, x in zip(row_blocks, hn):
        v = _dot(x, w_ref[:, POOL_W + 2 * ATT_W:POOL_W + 3 * ATT_W]).astype(_BF16)
        for pair in range(N_PAIRS):
            v_ref[rows, 2 * pair * LANES:(2 * pair + 1) * LANES] = v[:, pair * LANES:(pair + 1) * LANES]
            v_ref[rows, (2 * pair + 1) * LANES:(2 * pair + 2) * LANES] = jnp.ones((ROW_BLOCK, LANES), _BF16)


def _mix_in(h, layer, g, w_in, w_kt, pool_w, pool_b, pool_scale, tiles_per_seq):
    t, d = h.shape
    out = jax.ShapeDtypeStruct((t, ATT_W), _BF16)
    return pl.pallas_call(
        functools.partial(_mix_in_kernel, tiles_per_seq),
        grid=(t // TM,),
        in_specs=[
            _rows(d),
            _resident(g.shape, layer),
            _resident(w_in.shape, layer),
            _resident(w_kt.shape, layer),
            _resident(pool_w.shape, layer),
            _resident(pool_b.shape, layer),
            _resident(pool_scale.shape, layer),
        ],
        out_specs=[_rows(POOL_W), _rows(ATT_W), pl.BlockSpec((ATT_W, TM), lambda j: (0, j)), _rows(2 * ATT_W)],
        out_shape=[jax.ShapeDtypeStruct((t, POOL_W), _BF16), out,
                   jax.ShapeDtypeStruct((ATT_W, t), _BF16), jax.ShapeDtypeStruct((t, 2 * ATT_W), _BF16)],
        scratch_shapes=[pltpu.VMEM((POOL_HALO + TM, POOL_W), _F32)],
        compiler_params=pltpu.CompilerParams(
            dimension_semantics=("arbitrary",), vmem_limit_bytes=VMEM_LIMIT),
        name="mix_in",
    )(h, g, w_in, w_kt, pool_w, pool_b, pool_scale)


def _build_bias_tables(rb_ref, tbl):
    row = lax.broadcasted_iota(jnp.int32, (QSUB, KWIN), 0)
    col = lax.broadcasted_iota(jnp.int32, (QSUB, KWIN), 1)
    first_key = row & ~(CHUNK - 1)
    visible = (col >= first_key) & (col < first_key + (LEFT_CHUNKS + 1) * CHUNK)
    brow = lax.broadcasted_iota(jnp.int32, (QSUB, BAND), 0)
    bcol = lax.broadcasted_iota(jnp.int32, (QSUB, BAND), 1)
    for head in range(N_HEADS):
        r = rb_ref[head]
        far = r[:, 0:1]
        rolled = pltpu.roll(jnp.broadcast_to(r, (QSUB, BAND)), 0, 1, stride=1, stride_axis=0)
        band = jnp.where(bcol >= brow, rolled, far)
        full = jnp.concatenate([jnp.broadcast_to(far, (QSUB, KWIN - BAND)), band], axis=1)
        tbl[head // 2, (head % 2) * QSUB:(head % 2 + 1) * QSUB, :] = jnp.where(visible, full, NEG)
    tbl[NEG_TABLE] = jnp.full((2 * QSUB, KWIN), NEG, _F32)


def _attn_kernel(tiles_per_seq, q_ref, ktp_ref, ktc_ref, vp_ref, vc_ref, rb_ref, o_ref, tbl, s_s):
    @pl.when(pl.program_id(0) == 0)
    def _():
        _build_bias_tables(rb_ref, tbl)

    has_prev = (pl.program_id(0) % tiles_per_seq) != 0
    low_head = lax.broadcasted_iota(jnp.int32, (1, LANES), 1) < HEAD_DIM
    n_blocks = N_PAIRS * N_SUBS

    def geometry(n):
        pair, sub = divmod(n, N_SUBS)
        n_prev = TM - sub * QSUB
        return pair, slice(sub * QSUB, (sub + 1) * QSUB), slice(pair * LANES, (pair + 1) * LANES), n_prev

    def scores(n):
        pair, rows, cols, n_prev = geometry(n)
        q2 = q_ref[rows, cols]
        zero = jnp.zeros_like(q2)
        qq = jnp.concatenate([jnp.where(low_head, q2, zero), jnp.where(low_head, zero, q2)], axis=0)
        prev_tbl = jnp.where(has_prev, pair, NEG_TABLE)
        s_s[n % 2, :, 0:n_prev] = _dot(qq, ktp_ref[cols, TM - n_prev:TM]) + tbl[prev_tbl, :, 0:n_prev]
        s_s[n % 2, :, n_prev:KWIN] = _dot(qq, ktc_ref[cols, 0:KWIN - n_prev]) + tbl[pair, :, n_prev:KWIN]

    def outputs(n):
        pair, rows, cols, n_prev = geometry(n)
        vcols = slice(2 * pair * LANES, (2 * pair + 2) * LANES)
        m = jnp.max(s_s[n % 2], axis=-1, keepdims=True)
        e_prev = jnp.exp(s_s[n % 2, :, 0:n_prev] - m).astype(_BF16)
        e_cur = jnp.exp(s_s[n % 2, :, n_prev:KWIN] - m).astype(_BF16)
        ov = _dot(e_prev, vp_ref[TM - n_prev:TM, vcols]) + _dot(e_cur, vc_ref[0:KWIN - n_prev, vcols])
        o = ov[:, 0:LANES] / ov[:, LANES:2 * LANES]
        o_ref[rows, cols] = jnp.where(low_head, o[0:QSUB], o[QSUB:2 * QSUB]).astype(_BF16)

    scores(0)
    for n in range(n_blocks):
        if n + 1 < n_blocks:
            scores(n + 1)
        outputs(n)


def _attn(q, kt, v, layer, rb, tiles_per_seq):
    t, _ = q.shape

    def prev(j):
        return jnp.where(j % tiles_per_seq == 0, j, j - 1)

    return pl.pallas_call(
        functools.partial(_attn_kernel, tiles_per_seq),
        grid=(t // TM,),
        in_specs=[
            _rows(ATT_W),
            pl.BlockSpec((ATT_W, TM), lambda j: (0, prev(j))),
            pl.BlockSpec((ATT_W, TM), lambda j: (0, j)),
            pl.BlockSpec((TM, 2 * ATT_W), lambda j: (prev(j), 0)),
            _rows(2 * ATT_W),
            _resident(rb.shape, layer),
        ],
        out_specs=_rows(ATT_W),
        out_shape=jax.ShapeDtypeStruct((t, ATT_W), _BF16),
        scratch_shapes=[
            pltpu.VMEM((N_PAIRS + 1, 2 * QSUB, KWIN), _F32),
            pltpu.VMEM((2, 2 * QSUB, KWIN), _F32),
        ],
        compiler_params=pltpu.CompilerParams(
            dimension_semantics=("arbitrary",), vmem_limit_bytes=VMEM_LIMIT),
        name="attn",
    )(q, kt, kt, v, v, rb)


def _ff_chunks(d_ff):
    return tuple((off, min(FF_CHUNK, d_ff - off)) for off in range(0, d_ff, FF_CHUNK))


def _ffn_kernel(tiles_per_seq, d_ff, final, h_ref, a_ref, o_ref, p_ref, wout_ref, gf_ref, wup_ref, cw_ref,
                cb_ref, wdn_ref, gp_ref, wpg_ref, bpg_ref, wple_ref, fg_ref, out_ref,
                h1_s, hn_s, acc_s, gate_s, val_s, halo_s):
    @pl.when(pl.program_id(0) == 0)
    def _():
        halo_s[...] = jnp.zeros_like(halo_s)

    first = (pl.program_id(0) % tiles_per_seq) == 0
    chunks = _ff_chunks(d_ff)
    row_blocks = [slice(r, r + ROW_BLOCK) for r in range(0, TM, ROW_BLOCK)]

    def up(c, rows=slice(0, TM)):
        off, width = chunks[c]
        hn = hn_s[rows, :]
        gate_s[c % 2, CONV_HALO + rows.start:CONV_HALO + rows.stop, 0:width] = (
            _dot(hn, wup_ref[:, off:off + width]))
        val_s[c % 2, rows, 0:width] = _dot(hn, wup_ref[:, d_ff + off:d_ff + off + width])

    for rows in row_blocks:
        h1 = (h_ref[rows, :] + _dot(a_ref[rows, :], wout_ref[0:POOL_W, :])
              + _dot(o_ref[rows, :], wout_ref[POOL_W:POOL_W + ATT_W, :]))
        h1_s[rows, :] = h1
        hn_s[rows, :] = _rmsnorm(h1, gf_ref[...]).astype(_BF16)
    for rows in row_blocks:
        up(0, rows)

    def down(c):
        off, width = chunks[c]
        gate = gate_s.at[c % 2]
        gate[0:CONV_HALO, 0:width] = jnp.where(first, 0.0, halo_s[:, off:off + width])
        halo_s[:, off:off + width] = gate[TM:TM + CONV_HALO, 0:width]
        y = gate[CONV_HALO:CONV_HALO + TM, 0:width] * cw_ref[CONV_K - 1:CONV_K, off:off + width]
        y = y + cb_ref[:, off:off + width]
        for kk in range(CONV_K - 1):
            lag = CONV_K - 1 - kk
            y = y + gate[CONV_HALO - lag:CONV_HALO - lag + TM, 0:width] * cw_ref[kk:kk + 1, off:off + width]
        act = 0.5 * y * (1.0 + lax.erf(y * (1.0 / math.sqrt(2.0))))
        part = _dot((act * val_s[c % 2, :, 0:width]).astype(_BF16), wdn_ref[off:off + width, :])
        if c == 0:
            acc_s[...] = part
        else:
            acc_s[...] += part

    for c in range(len(chunks)):
        if c + 1 < len(chunks):
            up(c + 1)
        down(c)

    for rows in row_blocks:
        h2 = h1_s[rows, :] + acc_s[rows, :]
        hn3 = _rmsnorm(h2, gp_ref[...]).astype(_BF16)
        gate = jax.nn.sigmoid(_dot(hn3, wpg_ref[...]) + bpg_ref[...])
        h3 = h2 + gate * _dot(p_ref[rows, :].astype(_BF16), wple_ref[...])
        out_ref[rows, :] = _rmsnorm(h3, fg_ref[...]) if final else h3


def _ffn(h, a, o, layer, p, w_out, g_ffn, w_up, conv_w, conv_b, w_down, g_ple, w_pg, b_pg, w_ple, final_g,
         tiles_per_seq, final):
    t, d = h.shape
    d_ff = w_down.shape[1]
    per_layer = (w_out, g_ffn, w_up, conv_w, conv_b, w_down, g_ple, w_pg, b_pg, w_ple)
    return pl.pallas_call(
        functools.partial(_ffn_kernel, tiles_per_seq, d_ff, final),
        grid=(t // TM,),
        in_specs=[_rows(d), _rows(POOL_W), _rows(ATT_W),
                  pl.BlockSpec((None, TM, p.shape[2]), lambda j: (layer, j, 0))]
        + [_resident(w.shape, layer) for w in per_layer] + [_resident(final_g.shape)],
        out_specs=_rows(d),
        out_shape=jax.ShapeDtypeStruct((t, d), _F32),
        scratch_shapes=[
            pltpu.VMEM((TM, d), _F32),
            pltpu.VMEM((TM, d), _BF16),
            pltpu.VMEM((TM, d), _F32),
            pltpu.VMEM((2, CONV_HALO + TM, FF_CHUNK), _F32),
            pltpu.VMEM((2, TM, FF_CHUNK), _F32),
            pltpu.VMEM((CONV_HALO, d_ff), _F32),
        ],
        compiler_params=pltpu.CompilerParams(
            dimension_semantics=("arbitrary",), vmem_limit_bytes=VMEM_LIMIT),
        name="ffn",
    )(h, a, o, p, *per_layer, final_g)


def kernel(x, p, norm_mix_g, w_in, pool_w, pool_b, pool_scale, rel_bias, w_out, norm_ffn_g, w_up,
           conv_w, conv_b, w_down, norm_ple_g, w_ple_gate, b_ple_gate, w_ple, final_g):
    b, s, d = x.shape
    depth = w_in.shape[0]
    d_ff = w_down.shape[1]
    assert s % TM == 0 and d_ff % LANES == 0 and w_up.shape[2] == 2 * d_ff
    assert w_in.shape[2] == POOL_W + 3 * ATT_W and w_out.shape[1] == POOL_W + ATT_W
    assert rel_bias.shape[1:] == (N_HEADS, 2 * MAX_REL + 1)
    t = b * s
    tiles_per_seq = s // TM

    w_in_b = w_in.astype(_BF16)
    w_kt_b = w_in_b[:, :, POOL_W + ATT_W:POOL_W + 2 * ATT_W].transpose(0, 2, 1)
    pool_w_b = pool_w.astype(_BF16)
    w_out_b = w_out.astype(_BF16)
    w_up_b = w_up.astype(_BF16)
    w_down_b = w_down.astype(_BF16)
    w_pg_b = w_ple_gate.astype(_BF16)
    w_ple_b = w_ple.astype(_BF16)
    rb = rel_bias.astype(_F32)[:, :, :0:-1].reshape(depth, N_HEADS, 1, BAND)
    rows = lambda v: v.reshape(depth, 1, -1)
    p_rows = p.reshape(depth, t, -1)

    h = x.reshape(t, d)
    for i in range(depth):
        a, q, kt, v = _mix_in(h, i, rows(norm_mix_g), w_in_b, w_kt_b, pool_w_b, rows(pool_b), rows(pool_scale),
                              tiles_per_seq)
        o = _attn(q, kt, v, i, rb, tiles_per_seq)
        h = _ffn(h, a, o, i, p_rows, w_out_b, rows(norm_ffn_g), w_up_b, conv_w, rows(conv_b), w_down_b,
                 rows(norm_ple_g), w_pg_b, rows(b_ple_gate), w_ple_b, final_g.reshape(1, -1), tiles_per_seq,
                 final=(i == depth - 1))
    return h.reshape(b, s, d)
```

```python
import functools
import math

import jax
import jax.numpy as jnp
from jax import lax
from jax.experimental import pallas as pl
from jax.experimental.pallas import tpu as pltpu

CHUNK = 64
POOL_WINDOWS = (2, 4, 8, 16)
POOL_GROUP = 128
POOL_W = POOL_GROUP * len(POOL_WINDOWS)
N_HEADS = 8
HEAD_DIM = 64
ATT_W = N_HEADS * HEAD_DIM
LEFT_CHUNKS = 8
LEFT = LEFT_CHUNKS * CHUNK
MAX_REL = 128
CONV_K = 3
EPS = 1e-6

LANES = 128
SUBLANES = 8
TM = LEFT
QSUB = 2 * CHUNK
KWIN = QSUB + LEFT
N_PAIRS = N_HEADS * HEAD_DIM // LANES
N_SUBS = TM // QSUB
BAND = 2 * MAX_REL
NEG_TABLE = N_PAIRS
ROW_BLOCK = 256
POOL_HALO = 16
CONV_HALO = SUBLANES
FF_CHUNK = 768
NEG = -1e30
VMEM_LIMIT = 56 * 1024 * 1024

_BF16 = jnp.bfloat16
_F32 = jnp.float32


def _dot(a, b):
    return jnp.dot(a, b, preferred_element_type=_F32)


def _dot_t(a, b):
    return lax.dot_general(a, b, (((1,), (1,)), ((), ())), preferred_element_type=_F32)


def _rmsnorm(x, g):
    ms = jnp.mean(x * x, axis=-1, keepdims=True)
    return x * lax.rsqrt(ms + EPS) * g


def _resident(shape, layer=None):
    if layer is None:
        return pl.BlockSpec(shape, lambda j: (0,) * len(shape), pipeline_mode=pl.Buffered(1))
    return pl.BlockSpec((None,) + tuple(shape[1:]), lambda j: (layer,) + (0,) * (len(shape) - 1),
                        pipeline_mode=pl.Buffered(1))


def _rows(width):
    return pl.BlockSpec((TM, width), lambda j: (j, 0))


def _mix_in_kernel(tiles_per_seq, h_ref, g_ref, w_ref, wkt_ref, pw_ref, pb_ref, ps_ref,
                   a_ref, q_ref, kt_ref, v_ref, ubuf):
    @pl.when(pl.program_id(0) == 0)
    def _():
        ubuf[...] = jnp.zeros_like(ubuf)

    tile = pl.program_id(0) % tiles_per_seq
    ubuf[0:POOL_HALO, :] = jnp.where(tile == 0, 0.0, ubuf[TM:TM + POOL_HALO, :])

    row_blocks = [slice(r, r + ROW_BLOCK) for r in range(0, TM, ROW_BLOCK)]
    hn = [_rmsnorm(h_ref[rows, :], g_ref[...]).astype(_BF16) for rows in row_blocks]

    def pool(gi):
        win = POOL_WINDOWS[gi]
        cols = slice(gi * POOL_GROUP, (gi + 1) * POOL_GROUP)
        u = ubuf[POOL_HALO:POOL_HALO + TM, cols]
        s = u
        for d in range(1, win):
            s = s + ubuf[POOL_HALO - d:POOL_HALO - d + TM, cols]
        pos = tile * TM + lax.broadcasted_iota(jnp.int32, (TM, 1), 0)
        cnt = jnp.minimum(pos + 1, win).astype(_F32)
        y = (s / cnt - u).astype(_BF16)
        r = _dot(y, pw_ref[gi]) + pb_ref[:, cols]
        a_ref[:, cols] = (r * ps_ref[:, cols]).astype(_BF16)

    for rows, x in zip(row_blocks, hn):
        ubuf[POOL_HALO + rows.start:POOL_HALO + rows.stop, :] = _dot(x, w_ref[:, 0:POOL_W])
    pool(0)
    for rows, x in zip(row_blocks, hn):
        q = _dot(x, w_ref[:, POOL_W:POOL_W + ATT_W]) * (1.0 / math.sqrt(HEAD_DIM))
        q_ref[rows, :] = q.astype(_BF16)
    pool(1)
    for rows, x in zip(row_blocks, hn):
        kt_ref[:, rows] = _dot_t(wkt_ref[...], x).astype(_BF16)
    pool(2)
    pool(3)
    for rows, x in zip(row_blocks, hn):
        v = _dot(x, w_ref[:, POOL_W + 2 * ATT_W:POOL_W + 3 * ATT_W]).astype(_BF16)
        for pair in range(N_PAIRS):
            v_ref[rows, 2 * pair * LANES:(2 * pair + 1) * LANES] = v[:, pair * LANES:(pair + 1) * LANES]
            v_ref[rows, (2 * pair + 1) * LANES:(2 * pair + 2) * LANES] = jnp.ones((ROW_BLOCK, LANES), _BF16)


def _mix_in(h, layer, g, w_in, w_kt, pool_w, pool_b, pool_scale, tiles_per_seq):
    t, d = h.shape
    out = jax.ShapeDtypeStruct((t, ATT_W), _BF16)
    return pl.pallas_call(
        functools.partial(_mix_in_kernel, tiles_per_seq),
        grid=(t // TM,),
        in_specs=[
            _rows(d),
            _resident(g.shape, layer),
            _resident(w_in.shape, layer),
            _resident(w_kt.shape, layer),
            _resident(pool_w.shape, layer),
            _resident(pool_b.shape, layer),
            _resident(pool_scale.shape, layer),
        ],
        out_specs=[_rows(POOL_W), _rows(ATT_W), pl.BlockSpec((ATT_W, TM), lambda j: (0, j)), _rows(2 * ATT_W)],
        out_shape=[jax.ShapeDtypeStruct((t, POOL_W), _BF16), out,
                   jax.ShapeDtypeStruct((ATT_W, t), _BF16), jax.ShapeDtypeStruct((t, 2 * ATT_W), _BF16)],
        scratch_shapes=[pltpu.VMEM((POOL_HALO + TM, POOL_W), _F32)],
        compiler_params=pltpu.CompilerParams(
            dimension_semantics=("arbitrary",), vmem_limit_bytes=VMEM_LIMIT),
        name="mix_in",
    )(h, g, w_in, w_kt, pool_w, pool_b, pool_scale)


def _build_bias_tables(rb_ref, tbl):
    row = lax.broadcasted_iota(jnp.int32, (QSUB, KWIN), 0)
    col = lax.broadcasted_iota(jnp.int32, (QSUB, KWIN), 1)
    first_key = row & ~(CHUNK - 1)
    visible = (col >= first_key) & (col < first_key + (LEFT_CHUNKS + 1) * CHUNK)
    brow = lax.broadcasted_iota(jnp.int32, (QSUB, BAND), 0)
    bcol = lax.broadcasted_iota(jnp.int32, (QSUB, BAND), 1)
    for head in range(N_HEADS):
        r = rb_ref[head]
        far = r[:, 0:1]
        rolled = pltpu.roll(jnp.broadcast_to(r, (QSUB, BAND)), 0, 1, stride=1, stride_axis=0)
        band = jnp.where(bcol >= brow, rolled, far)
        full = jnp.concatenate([jnp.broadcast_to(far, (QSUB, KWIN - BAND)), band], axis=1)
        tbl[head // 2, (head % 2) * QSUB:(head % 2 + 1) * QSUB, :] = jnp.where(visible, full, NEG)
    tbl[NEG_TABLE] = jnp.full((2 * QSUB, KWIN), NEG, _F32)


def _attn_kernel(tiles_per_seq, q_ref, ktp_ref, ktc_ref, vp_ref, vc_ref, rb_ref, o_ref, tbl, s_s, m_s):
    @pl.when(pl.program_id(0) == 0)
    def _():
        _build_bias_tables(rb_ref, tbl)

    has_prev = (pl.program_id(0) % tiles_per_seq) != 0
    low_head = lax.broadcasted_iota(jnp.int32, (1, LANES), 1) < HEAD_DIM
    n_blocks = N_PAIRS * N_SUBS

    def geometry(n):
        pair, sub = divmod(n, N_SUBS)
        n_prev = TM - sub * QSUB
        return pair, slice(sub * QSUB, (sub + 1) * QSUB), slice(pair * LANES, (pair + 1) * LANES), n_prev

    def scores(n):
        pair, rows, cols, n_prev = geometry(n)
        q2 = q_ref[rows, cols]
        zero = jnp.zeros_like(q2)
        qq = jnp.concatenate([jnp.where(low_head, q2, zero), jnp.where(low_head, zero, q2)], axis=0)
        prev_tbl = jnp.where(has_prev, pair, NEG_TABLE)
        s_prev = _dot(qq, ktp_ref[cols, TM - n_prev:TM]) + tbl[prev_tbl, :, 0:n_prev]
        s_cur = _dot(qq, ktc_ref[cols, 0:KWIN - n_prev]) + tbl[pair, :, n_prev:KWIN]
        s_s[n % 2, :, 0:n_prev] = s_prev
        s_s[n % 2, :, n_prev:KWIN] = s_cur
        m = jnp.maximum(jnp.max(s_prev, axis=-1, keepdims=True), jnp.max(s_cur, axis=-1, keepdims=True))
        m_s[n % 2] = jnp.broadcast_to(m, (2 * QSUB, LANES))

    def outputs(n):
        pair, rows, cols, n_prev = geometry(n)
        vcols = slice(2 * pair * LANES, (2 * pair + 2) * LANES)
        m = m_s[n % 2]

        def weights(lo, hi):
            return jnp.concatenate([jnp.exp(s_s[n % 2, :, c:c + LANES] - m).astype(_BF16)
                                    for c in range(lo, hi, LANES)], axis=1)

        e_prev = weights(0, n_prev)
        e_cur = weights(n_prev, KWIN)
        ov = _dot(e_prev, vp_ref[TM - n_prev:TM, vcols]) + _dot(e_cur, vc_ref[0:KWIN - n_prev, vcols])
        o = ov[:, 0:LANES] / ov[:, LANES:2 * LANES]
        o_ref[rows, cols] = jnp.where(low_head, o[0:QSUB], o[QSUB:2 * QSUB]).astype(_BF16)

    scores(0)
    for n in range(n_blocks):
        if n + 1 < n_blocks:
            scores(n + 1)
        outputs(n)


def _attn(q, kt, v, layer, rb, tiles_per_seq):
    t, _ = q.shape

    def prev(j):
        return jnp.where(j % tiles_per_seq == 0, j, j - 1)

    return pl.pallas_call(
        functools.partial(_attn_kernel, tiles_per_seq),
        grid=(t // TM,),
        in_specs=[
            _rows(ATT_W),
            pl.BlockSpec((ATT_W, TM), lambda j: (0, prev(j))),
            pl.BlockSpec((ATT_W, TM), lambda j: (0, j)),
            pl.BlockSpec((TM, 2 * ATT_W), lambda j: (prev(j), 0)),
            _rows(2 * ATT_W),
            _resident(rb.shape, layer),
        ],
        out_specs=_rows(ATT_W),
        out_shape=jax.ShapeDtypeStruct((t, ATT_W), _BF16),
        scratch_shapes=[
            pltpu.VMEM((N_PAIRS + 1, 2 * QSUB, KWIN), _F32),
            pltpu.VMEM((2, 2 * QSUB, KWIN), _F32),
            pltpu.VMEM((2, 2 * QSUB, LANES), _F32),
        ],
        compiler_params=pltpu.CompilerParams(
            dimension_semantics=("arbitrary",), vmem_limit_bytes=VMEM_LIMIT),
        name="attn",
    )(q, kt, kt, v, v, rb)


def _ff_chunks(d_ff):
    return tuple((off, min(FF_CHUNK, d_ff - off)) for off in range(0, d_ff, FF_CHUNK))


def _ffn_kernel(tiles_per_seq, d_ff, final, h_ref, a_ref, o_ref, p_ref, wout_ref, gf_ref, wup_ref, cw_ref,
                cb_ref, wdn_ref, gp_ref, wpg_ref, bpg_ref, wple_ref, fg_ref, out_ref,
                h1_s, hn_s, acc_s, gate_s, val_s, halo_s):
    @pl.when(pl.program_id(0) == 0)
    def _():
        halo_s[...] = jnp.zeros_like(halo_s)

    first = (pl.program_id(0) % tiles_per_seq) == 0
    chunks = _ff_chunks(d_ff)
    row_blocks = [slice(r, r + ROW_BLOCK) for r in range(0, TM, ROW_BLOCK)]

    def up(c, rows=slice(0, TM)):
        off, width = chunks[c]
        hn = hn_s[rows, :]
        gate_s[c % 2, CONV_HALO + rows.start:CONV_HALO + rows.stop, 0:width] = (
            _dot(hn, wup_ref[:, off:off + width]))
        val_s[c % 2, rows, 0:width] = _dot(hn, wup_ref[:, d_ff + off:d_ff + off + width])

    for rows in row_blocks:
        h1 = (h_ref[rows, :] + _dot(a_ref[rows, :], wout_ref[0:POOL_W, :])
              + _dot(o_ref[rows, :], wout_ref[POOL_W:POOL_W + ATT_W, :]))
        h1_s[rows, :] = h1
        hn_s[rows, :] = _rmsnorm(h1, gf_ref[...]).astype(_BF16)
    for rows in row_blocks:
        up(0, rows)

    def down(c):
        off, width = chunks[c]
        gate = gate_s.at[c % 2]
        gate[0:CONV_HALO, 0:width] = jnp.where(first, 0.0, halo_s[:, off:off + width])
        halo_s[:, off:off + width] = gate[TM:TM + CONV_HALO, 0:width]
        y = gate[CONV_HALO:CONV_HALO + TM, 0:width] * cw_ref[CONV_K - 1:CONV_K, off:off + width]
        y = y + cb_ref[:, off:off + width]
        for kk in range(CONV_K - 1):
            lag = CONV_K - 1 - kk
            y = y + gate[CONV_HALO - lag:CONV_HALO - lag + TM, 0:width] * cw_ref[kk:kk + 1, off:off + width]
        act = 0.5 * y * (1.0 + lax.erf(y * (1.0 / math.sqrt(2.0))))
        part = _dot((act * val_s[c % 2, :, 0:width]).astype(_BF16), wdn_ref[off:off + width, :])
        if c == 0:
            acc_s[...] = part
        else:
            acc_s[...] += part

    for c in range(len(chunks)):
        if c + 1 < len(chunks):
            up(c + 1)
        down(c)

    for rows in row_blocks:
        h2 = h1_s[rows, :] + acc_s[rows, :]
        hn3 = _rmsnorm(h2, gp_ref[...]).astype(_BF16)
        gate = jax.nn.sigmoid(_dot(hn3, wpg_ref[...]) + bpg_ref[...])
        h3 = h2 + gate * _dot(p_ref[rows, :].astype(_BF16), wple_ref[...])
        out_ref[rows, :] = _rmsnorm(h3, fg_ref[...]) if final else h3


def _ffn(h, a, o, layer, p, w_out, g_ffn, w_up, conv_w, conv_b, w_down, g_ple, w_pg, b_pg, w_ple, final_g,
         tiles_per_seq, final):
    t, d = h.shape
    d_ff = w_down.shape[1]
    per_layer = (w_out, g_ffn, w_up, conv_w, conv_b, w_down, g_ple, w_pg, b_pg, w_ple)
    return pl.pallas_call(
        functools.partial(_ffn_kernel, tiles_per_seq, d_ff, final),
        grid=(t // TM,),
        in_specs=[_rows(d), _rows(POOL_W), _rows(ATT_W),
                  pl.BlockSpec((None, TM, p.shape[2]), lambda j: (layer, j, 0))]
        + [_resident(w.shape, layer) for w in per_layer] + [_resident(final_g.shape)],
        out_specs=_rows(d),
        out_shape=jax.ShapeDtypeStruct((t, d), _F32),
        scratch_shapes=[
            pltpu.VMEM((TM, d), _F32),
            pltpu.VMEM((TM, d), _BF16),
            pltpu.VMEM((TM, d), _F32),
            pltpu.VMEM((2, CONV_HALO + TM, FF_CHUNK), _F32),
            pltpu.VMEM((2, TM, FF_CHUNK), _F32),
            pltpu.VMEM((CONV_HALO, d_ff), _F32),
        ],
        compiler_params=pltpu.CompilerParams(
            dimension_semantics=("arbitrary",), vmem_limit_bytes=VMEM_LIMIT),
        name="ffn",
    )(h, a, o, p, *per_layer, final_g)


def kernel(x, p, norm_mix_g, w_in, pool_w, pool_b, pool_scale, rel_bias, w_out, norm_ffn_g, w_up,
           conv_w, conv_b, w_down, norm_ple_g, w_ple_gate, b_ple_gate, w_ple, final_g):
    b, s, d = x.shape
    depth = w_in.shape[0]
    d_ff = w_down.shape[1]
    assert s % TM == 0 and d_ff % LANES == 0 and w_up.shape[2] == 2 * d_ff
    assert w_in.shape[2] == POOL_W + 3 * ATT_W and w_out.shape[1] == POOL_W + ATT_W
    assert rel_bias.shape[1:] == (N_HEADS, 2 * MAX_REL + 1)
    t = b * s
    tiles_per_seq = s // TM

    w_in_b = w_in.astype(_BF16)
    w_k = lax.optimization_barrier(w_in[:, :, POOL_W + ATT_W:POOL_W + 2 * ATT_W])
    w_kt_b = w_k.transpose(0, 2, 1).astype(_BF16)
    pool_w_b = pool_w.astype(_BF16)
    w_out_b = w_out.astype(_BF16)
    w_up_b = w_up.astype(_BF16)
    w_down_b = w_down.astype(_BF16)
    w_pg_b = w_ple_gate.astype(_BF16)
    w_ple_b = w_ple.astype(_BF16)
    rb = rel_bias.astype(_F32)[:, :, :0:-1].reshape(depth, N_HEADS, 1, BAND)
    rows = lambda v: v.reshape(depth, 1, -1)
    p_rows = p.reshape(depth, t, -1)

    h = x.reshape(t, d)
    for i in range(depth):
        a, q, kt, v = _mix_in(h, i, rows(norm_mix_g), w_in_b, w_kt_b, pool_w_b, rows(pool_b), rows(pool_scale),
                              tiles_per_seq)
        o = _attn(q, kt, v, i, rb, tiles_per_seq)
        h = _ffn(h, a, o, i, p_rows, w_out_b, rows(norm_ffn_g), w_up_b, conv_w, rows(conv_b), w_down_b,
                 rows(norm_ple_g), w_pg_b, rows(b_ple_gate), w_ple_b, final_g.reshape(1, -1), tiles_per_seq,
                 final=(i == depth - 1))
    return h.reshape(b, s, d)
```

```python
import functools
import math

import jax
import jax.numpy as jnp
from jax import lax
from jax.experimental import pallas as pl
from jax.experimental.pallas import tpu as pltpu

CHUNK = 64
POOL_WINDOWS = (2, 4, 8, 16)
POOL_GROUP = 128
POOL_W = POOL_GROUP * len(POOL_WINDOWS)
N_HEADS = 8
HEAD_DIM = 64
ATT_W = N_HEADS * HEAD_DIM
LEFT_CHUNKS = 8
LEFT = LEFT_CHUNKS * CHUNK
MAX_REL = 128
CONV_K = 3
EPS = 1e-6

LANES = 128
SUBLANES = 8
TM = LEFT
QSUB = 2 * CHUNK
KWIN = QSUB + LEFT
N_PAIRS = N_HEADS * HEAD_DIM // LANES
N_SUBS = TM // QSUB
BAND = 2 * MAX_REL
NEG_TABLE = N_PAIRS
ROW_BLOCK = 256
COL_BLOCK = 256
RING = 3
POOL_HALO = 16
CONV_HALO = SUBLANES
FF_CHUNK = 768
NEG = -1e30
VMEM_LIMIT = 56 * 1024 * 1024

_BF16 = jnp.bfloat16
_F32 = jnp.float32


def _dot(a, b):
    return jnp.dot(a, b, preferred_element_type=_F32)


def _dot_t(a, b):
    return lax.dot_general(a, b, (((1,), (1,)), ((), ())), preferred_element_type=_F32)


def _rmsnorm(x, g):
    ms = jnp.mean(x * x, axis=-1, keepdims=True)
    return x * lax.rsqrt(ms + EPS) * g


def _resident(shape, layer=None):
    if layer is None:
        return pl.BlockSpec(shape, lambda j: (0,) * len(shape), pipeline_mode=pl.Buffered(1))
    return pl.BlockSpec((None,) + tuple(shape[1:]), lambda j: (layer,) + (0,) * (len(shape) - 1),
                        pipeline_mode=pl.Buffered(1))


def _rows(width):
    return pl.BlockSpec((TM, width), lambda j: (j, 0))


def _interleave(xs, ys):
    out, taken = [], 0
    for i, x in enumerate(xs):
        out.append(x)
        want = (i + 1) * len(ys) // len(xs)
        out += ys[taken:want]
        taken = want
    return out


def _mix_in_pieces(tile, h_ref, g_ref, w_ref, wkt_ref, pw_ref, pb_ref, ps_ref, a_ref, ubuf, q_new, kt_new, v_new):
    row_blocks = [slice(r, r + ROW_BLOCK) for r in range(0, TM, ROW_BLOCK)]
    hn = {}

    def norm(rows):
        hn[rows.start] = _rmsnorm(h_ref[rows, :], g_ref[...]).astype(_BF16)

    def proj_u(rows, cols):
        ubuf[POOL_HALO + rows.start:POOL_HALO + rows.stop, cols] = _dot(hn[rows.start], w_ref[:, cols])

    def proj_q(rows, cols):
        q = _dot(hn[rows.start], w_ref[:, POOL_W + cols.start:POOL_W + cols.stop]) * (1.0 / math.sqrt(HEAD_DIM))
        q_new[rows, cols] = q.astype(_BF16)

    def proj_k(rows, cols):
        kt_new[cols, rows] = _dot_t(wkt_ref[cols, :], hn[rows.start]).astype(_BF16)

    def proj_v(rows, cols):
        base = POOL_W + 2 * ATT_W
        v = _dot(hn[rows.start], w_ref[:, base + cols.start:base + cols.stop]).astype(_BF16)
        for i in range(COL_BLOCK // LANES):
            pair = cols.start // LANES + i
            v_new[rows, 2 * pair * LANES:(2 * pair + 1) * LANES] = v[:, i * LANES:(i + 1) * LANES]
            v_new[rows, (2 * pair + 1) * LANES:(2 * pair + 2) * LANES] = jnp.ones((ROW_BLOCK, LANES), _BF16)

    def pool(gi):
        win = POOL_WINDOWS[gi]
        cols = slice(gi * POOL_GROUP, (gi + 1) * POOL_GROUP)
        u = ubuf[POOL_HALO:POOL_HALO + TM, cols]
        s = u
        for d in range(1, win):
            s = s + ubuf[POOL_HALO - d:POOL_HALO - d + TM, cols]
        pos = tile * TM + lax.broadcasted_iota(jnp.int32, (TM, 1), 0)
        cnt = jnp.minimum(pos + 1, win).astype(_F32)
        y = (s / cnt - u).astype(_BF16)
        r = _dot(y, pw_ref[gi]) + pb_ref[:, cols]
        a_ref[:, cols] = (r * ps_ref[:, cols]).astype(_BF16)

    part = functools.partial
    col_blocks = [slice(c, c + COL_BLOCK) for c in range(0, ATT_W, COL_BLOCK)]
    norms = [part(norm, rows) for rows in row_blocks]
    projections = [part(proj, rows, cols) for proj in (proj_u, proj_q, proj_k, proj_v)
                   for rows in row_blocks for cols in col_blocks]
    pools = [part(pool, gi) for gi in range(len(POOL_WINDOWS))]
    return norms, projections, pools


def _build_bias_tables(rb_ref, tbl):
    row = lax.broadcasted_iota(jnp.int32, (QSUB, KWIN), 0)
    col = lax.broadcasted_iota(jnp.int32, (QSUB, KWIN), 1)
    first_key = row & ~(CHUNK - 1)
    visible = (col >= first_key) & (col < first_key + (LEFT_CHUNKS + 1) * CHUNK)
    brow = lax.broadcasted_iota(jnp.int32, (QSUB, BAND), 0)
    bcol = lax.broadcasted_iota(jnp.int32, (QSUB, BAND), 1)
    for head in range(N_HEADS):
        r = rb_ref[head]
        far = r[:, 0:1]
        rolled = pltpu.roll(jnp.broadcast_to(r, (QSUB, BAND)), 0, 1, stride=1, stride_axis=0)
        band = jnp.where(bcol >= brow, rolled, far)
        full = jnp.concatenate([jnp.broadcast_to(far, (QSUB, KWIN - BAND)), band], axis=1)
        tbl[head // 2, (head % 2) * QSUB:(head % 2 + 1) * QSUB, :] = jnp.where(visible, full, NEG)
    tbl[NEG_TABLE] = jnp.full((2 * QSUB, KWIN), NEG, _F32)


def _attention_pieces(has_prev, q_ref, ktp_ref, ktc_ref, vp_ref, vc_ref, tbl, s_s, m_s, o_ref):
    low_head = lax.broadcasted_iota(jnp.int32, (1, LANES), 1) < HEAD_DIM
    n_blocks = N_PAIRS * N_SUBS

    def geometry(n):
        pair, sub = divmod(n, N_SUBS)
        n_prev = TM - sub * QSUB
        return pair, slice(sub * QSUB, (sub + 1) * QSUB), slice(pair * LANES, (pair + 1) * LANES), n_prev

    def scores(n):
        pair, rows, cols, n_prev = geometry(n)
        q2 = q_ref[rows, cols]
        zero = jnp.zeros_like(q2)
        qq = jnp.concatenate([jnp.where(low_head, q2, zero), jnp.where(low_head, zero, q2)], axis=0)
        prev_tbl = jnp.where(has_prev, pair, NEG_TABLE)
        s_prev = _dot(qq, ktp_ref[cols, TM - n_prev:TM]) + tbl[prev_tbl, :, 0:n_prev]
        s_cur = _dot(qq, ktc_ref[cols, 0:KWIN - n_prev]) + tbl[pair, :, n_prev:KWIN]
        s_s[n % 2, :, 0:n_prev] = s_prev
        s_s[n % 2, :, n_prev:KWIN] = s_cur
        m = jnp.maximum(jnp.max(s_prev, axis=-1, keepdims=True), jnp.max(s_cur, axis=-1, keepdims=True))
        m_s[n % 2] = jnp.broadcast_to(m, (2 * QSUB, LANES))

    e = {}

    def weights(n):
        _, _, _, n_prev = geometry(n)
        m = m_s[n % 2]

        def exps(lo, hi):
            return jnp.concatenate([jnp.exp(s_s[n % 2, :, c:c + LANES] - m).astype(_BF16)
                                    for c in range(lo, hi, LANES)], axis=1)

        e[n] = (exps(0, n_prev), exps(n_prev, KWIN))

    def outputs(n):
        pair, rows, cols, n_prev = geometry(n)
        vcols = slice(2 * pair * LANES, (2 * pair + 2) * LANES)
        e_prev, e_cur = e.pop(n)
        ov = _dot(e_prev, vp_ref[TM - n_prev:TM, vcols]) + _dot(e_cur, vc_ref[0:KWIN - n_prev, vcols])
        o = ov[:, 0:LANES] / ov[:, LANES:2 * LANES]
        o_ref[rows, cols] = jnp.where(low_head, o[0:QSUB], o[QSUB:2 * QSUB]).astype(_BF16)

    part = functools.partial
    return [(part(scores, n), part(weights, n), part(outputs, n)) for n in range(n_blocks)]


def _mix_attn_kernel(tiles_per_seq, h_ref, g_ref, w_ref, wkt_ref, pw_ref, pb_ref, ps_ref, rb_ref,
                     a_ref, o_ref,
                     ubuf, q_new, kt_new, v_new, q_ring, kt_ring, v_ring, tbl, s_s, m_s):
    step = pl.program_id(0)

    @pl.when(step == 0)
    def _():
        _build_bias_tables(rb_ref, tbl)
        ubuf[...] = jnp.zeros_like(ubuf)
        q_ring[...] = jnp.zeros_like(q_ring)
        kt_ring[...] = jnp.zeros_like(kt_ring)
        v_ring[...] = jnp.ones_like(v_ring)

    tile = step % tiles_per_seq
    has_prev = ((step - 1) % tiles_per_seq) != 0
    ubuf[0:POOL_HALO, :] = jnp.where(tile == 0, 0.0, ubuf[TM:TM + POOL_HALO, :])

    norms, projections, pools = _mix_in_pieces(tile, h_ref, g_ref, w_ref, wkt_ref, pw_ref, pb_ref, ps_ref,
                                               a_ref, ubuf, q_new, kt_new, v_new)
    prev_slot, cur_slot = (step + 1) % RING, (step + 2) % RING
    attn = _attention_pieces(has_prev, q_ring.at[(step + 1) % 2], kt_ring.at[prev_slot], kt_ring.at[cur_slot],
                             v_ring.at[prev_slot], v_ring.at[cur_slot], tbl, s_s, m_s, o_ref)
    for run in norms:
        run()
    n_u = len(projections) // 4
    queue = projections[:n_u] + _interleave(projections[n_u:], pools)
    attn[0][0]()
    for n, (_, weights, outputs) in enumerate(attn):
        weights()
        if n + 1 < len(attn):
            attn[n + 1][0]()
        for run in queue[n * len(queue) // len(attn):(n + 1) * len(queue) // len(attn)]:
            run()
        outputs()

    kt_ring[step % RING] = kt_new[...]
    v_ring[step % RING] = v_new[...]
    q_ring[step % 2] = q_new[...]


def _mix_attn(h, layer, g, w_in, w_kt, pool_w, pool_b, pool_scale, rb, tiles_per_seq):
    t, d = h.shape
    n_tiles = t // TM
    return pl.pallas_call(
        functools.partial(_mix_attn_kernel, tiles_per_seq),
        grid=(n_tiles + 1,),
        in_specs=[
            pl.BlockSpec((TM, d), lambda j: (jnp.minimum(j, n_tiles - 1), 0)),
            _resident(g.shape, layer),
            _resident(w_in.shape, layer),
            _resident(w_kt.shape, layer),
            _resident(pool_w.shape, layer),
            _resident(pool_b.shape, layer),
            _resident(pool_scale.shape, layer),
            _resident(rb.shape, layer),
        ],
        out_specs=[_rows(POOL_W), pl.BlockSpec((TM, ATT_W), lambda j: (jnp.maximum(j - 1, 0), 0))],
        out_shape=[jax.ShapeDtypeStruct((t + TM, POOL_W), _BF16), jax.ShapeDtypeStruct((t, ATT_W), _BF16)],
        scratch_shapes=[
            pltpu.VMEM((POOL_HALO + TM, POOL_W), _F32),
            pltpu.VMEM((TM, ATT_W), _BF16),
            pltpu.VMEM((ATT_W, TM), _BF16),
            pltpu.VMEM((TM, 2 * ATT_W), _BF16),
            pltpu.VMEM((2, TM, ATT_W), _BF16),
            pltpu.VMEM((RING, ATT_W, TM), _BF16),
            pltpu.VMEM((RING, TM, 2 * ATT_W), _BF16),
            pltpu.VMEM((N_PAIRS + 1, 2 * QSUB, KWIN), _F32),
            pltpu.VMEM((2, 2 * QSUB, KWIN), _F32),
            pltpu.VMEM((2, 2 * QSUB, LANES), _F32),
        ],
        compiler_params=pltpu.CompilerParams(
            dimension_semantics=("arbitrary",), vmem_limit_bytes=VMEM_LIMIT),
        name="mix_attn",
    )(h, g, w_in, w_kt, pool_w, pool_b, pool_scale, rb)


def _ff_chunks(d_ff):
    return tuple((off, min(FF_CHUNK, d_ff - off)) for off in range(0, d_ff, FF_CHUNK))


def _ffn_kernel(tiles_per_seq, d_ff, final, h_ref, a_ref, o_ref, p_ref, wout_ref, gf_ref, wup_ref, cw_ref,
                cb_ref, wdn_ref, gp_ref, wpg_ref, bpg_ref, wple_ref, fg_ref, out_ref,
                h1_s, hn_s, acc_s, gate_s, val_s, halo_s):
    @pl.when(pl.program_id(0) == 0)
    def _():
        halo_s[...] = jnp.zeros_like(halo_s)

    first = (pl.program_id(0) % tiles_per_seq) == 0
    chunks = _ff_chunks(d_ff)
    row_blocks = [slice(r, r + ROW_BLOCK) for r in range(0, TM, ROW_BLOCK)]

    def up(c, rows=slice(0, TM)):
        off, width = chunks[c]
        hn = hn_s[rows, :]
        gate_s[c % 2, CONV_HALO + rows.start:CONV_HALO + rows.stop, 0:width] = (
            _dot(hn, wup_ref[:, off:off + width]))
        val_s[c % 2, rows, 0:width] = _dot(hn, wup_ref[:, d_ff + off:d_ff + off + width])

    for rows in row_blocks:
        h1 = (h_ref[rows, :] + _dot(a_ref[rows, :], wout_ref[0:POOL_W, :])
              + _dot(o_ref[rows, :], wout_ref[POOL_W:POOL_W + ATT_W, :]))
        h1_s[rows, :] = h1
        hn_s[rows, :] = _rmsnorm(h1, gf_ref[...]).astype(_BF16)
    for rows in row_blocks:
        up(0, rows)

    def down(c):
        off, width = chunks[c]
        gate = gate_s.at[c % 2]
        gate[0:CONV_HALO, 0:width] = jnp.where(first, 0.0, halo_s[:, off:off + width])
        halo_s[:, off:off + width] = gate[TM:TM + CONV_HALO, 0:width]
        y = gate[CONV_HALO:CONV_HALO + TM, 0:width] * cw_ref[CONV_K - 1:CONV_K, off:off + width]
        y = y + cb_ref[:, off:off + width]
        for kk in range(CONV_K - 1):
            lag = CONV_K - 1 - kk
            y = y + gate[CONV_HALO - lag:CONV_HALO - lag + TM, 0:width] * cw_ref[kk:kk + 1, off:off + width]
        act = 0.5 * y * (1.0 + lax.erf(y * (1.0 / math.sqrt(2.0))))
        part = _dot((act * val_s[c % 2, :, 0:width]).astype(_BF16), wdn_ref[off:off + width, :])
        if c == 0:
            acc_s[...] = part
        else:
            acc_s[...] += part

    for c in range(len(chunks)):
        if c + 1 < len(chunks):
            up(c + 1)
        down(c)

    for rows in row_blocks:
        h2 = h1_s[rows, :] + acc_s[rows, :]
        hn3 = _rmsnorm(h2, gp_ref[...]).astype(_BF16)
        gate = jax.nn.sigmoid(_dot(hn3, wpg_ref[...]) + bpg_ref[...])
        h3 = h2 + gate * _dot(p_ref[rows, :].astype(_BF16), wple_ref[...])
        out_ref[rows, :] = _rmsnorm(h3, fg_ref[...]) if final else h3


def _ffn(h, a, o, layer, p, w_out, g_ffn, w_up, conv_w, conv_b, w_down, g_ple, w_pg, b_pg, w_ple, final_g,
         tiles_per_seq, final):
    t, d = h.shape
    d_ff = w_down.shape[1]
    per_layer = (w_out, g_ffn, w_up, conv_w, conv_b, w_down, g_ple, w_pg, b_pg, w_ple)
    return pl.pallas_call(
        functools.partial(_ffn_kernel, tiles_per_seq, d_ff, final),
        grid=(t // TM,),
        in_specs=[_rows(d), _rows(POOL_W), _rows(ATT_W),
                  pl.BlockSpec((None, TM, p.shape[2]), lambda j: (layer, j, 0))]
        + [_resident(w.shape, layer) for w in per_layer] + [_resident(final_g.shape)],
        out_specs=_rows(d),
        out_shape=jax.ShapeDtypeStruct((t, d), _F32),
        scratch_shapes=[
            pltpu.VMEM((TM, d), _F32),
            pltpu.VMEM((TM, d), _BF16),
            pltpu.VMEM((TM, d), _F32),
            pltpu.VMEM((2, CONV_HALO + TM, FF_CHUNK), _F32),
            pltpu.VMEM((2, TM, FF_CHUNK), _F32),
            pltpu.VMEM((CONV_HALO, d_ff), _F32),
        ],
        compiler_params=pltpu.CompilerParams(
            dimension_semantics=("arbitrary",), vmem_limit_bytes=VMEM_LIMIT),
        name="ffn",
    )(h, a, o, p, *per_layer, final_g)


def kernel(x, p, norm_mix_g, w_in, pool_w, pool_b, pool_scale, rel_bias, w_out, norm_ffn_g, w_up,
           conv_w, conv_b, w_down, norm_ple_g, w_ple_gate, b_ple_gate, w_ple, final_g):
    b, s, d = x.shape
    depth = w_in.shape[0]
    d_ff = w_down.shape[1]
    assert s % TM == 0 and d_ff % LANES == 0 and w_up.shape[2] == 2 * d_ff
    assert w_in.shape[2] == POOL_W + 3 * ATT_W and w_out.shape[1] == POOL_W + ATT_W
    assert rel_bias.shape[1:] == (N_HEADS, 2 * MAX_REL + 1)
    t = b * s
    tiles_per_seq = s // TM

    w_in_b = w_in.astype(_BF16)
    w_k = lax.optimization_barrier(w_in[:, :, POOL_W + ATT_W:POOL_W + 2 * ATT_W])
    w_kt_b = w_k.transpose(0, 2, 1).astype(_BF16)
    pool_w_b = pool_w.astype(_BF16)
    w_out_b = w_out.astype(_BF16)
    w_up_b = w_up.astype(_BF16)
    w_down_b = w_down.astype(_BF16)
    w_pg_b = w_ple_gate.astype(_BF16)
    w_ple_b = w_ple.astype(_BF16)
    rb = rel_bias.astype(_F32)[:, :, :0:-1].reshape(depth, N_HEADS, 1, BAND)
    rows = lambda v: v.reshape(depth, 1, -1)
    p_rows = p.reshape(depth, t, -1)

    h = x.reshape(t, d)
    for i in range(depth):
        a, o = _mix_attn(h, i, rows(norm_mix_g), w_in_b, w_kt_b, pool_w_b, rows(pool_b), rows(pool_scale), rb,
                         tiles_per_seq)
        h = _ffn(h, a, o, i, p_rows, w_out_b, rows(norm_ffn_g), w_up_b, conv_w, rows(conv_b), w_down_b,
                 rows(norm_ple_g), w_pg_b, rows(b_ple_gate), w_ple_b, final_g.reshape(1, -1), tiles_per_seq,
                 final=(i == depth - 1))
    return h.reshape(b, s, d)
```

```python
import functools
import math

import jax
import jax.numpy as jnp
from jax import lax
from jax.experimental import pallas as pl
from jax.experimental.pallas import tpu as pltpu

CHUNK = 64
POOL_WINDOWS = (2, 4, 8, 16)
POOL_GROUP = 128
POOL_W = POOL_GROUP * len(POOL_WINDOWS)
N_HEADS = 8
HEAD_DIM = 64
ATT_W = N_HEADS * HEAD_DIM
LEFT_CHUNKS = 8
LEFT = LEFT_CHUNKS * CHUNK
MAX_REL = 128
CONV_K = 3
EPS = 1e-6

LANES = 128
SUBLANES = 8
TM = LEFT
QSUB = 2 * CHUNK
KWIN = QSUB + LEFT
N_PAIRS = N_HEADS * HEAD_DIM // LANES
N_SUBS = TM // QSUB
BAND = 2 * MAX_REL
NEG_TABLE = N_PAIRS
ROW_BLOCK = 256
COL_BLOCK = 256
POOL_HALO = 16
CONV_HALO = SUBLANES
FF_CHUNK = 768
NEG = -1e30
VMEM_LIMIT = 56 * 1024 * 1024

_BF16 = jnp.bfloat16
_F32 = jnp.float32


def _dot(a, b):
    return jnp.dot(a, b, preferred_element_type=_F32)


def _dot_t(a, b):
    return lax.dot_general(a, b, (((1,), (1,)), ((), ())), preferred_element_type=_F32)


def _rmsnorm(x, g):
    ms = jnp.mean(x * x, axis=-1, keepdims=True)
    return x * lax.rsqrt(ms + EPS) * g


def _resident(shape, layer=None):
    if layer is None:
        return pl.BlockSpec(shape, lambda j: (0,) * len(shape), pipeline_mode=pl.Buffered(1))
    return pl.BlockSpec((None,) + tuple(shape[1:]), lambda j: (layer,) + (0,) * (len(shape) - 1),
                        pipeline_mode=pl.Buffered(1))


def _rows(width):
    return pl.BlockSpec((TM, width), lambda j: (j, 0))


def _interleave(xs, ys):
    out, taken = [], 0
    for i, x in enumerate(xs):
        out.append(x)
        want = (i + 1) * len(ys) // len(xs)
        out += ys[taken:want]
        taken = want
    return out


def _mix_in_pieces(tile, h_ref, g_ref, w_ref, wkt_ref, pw_ref, pb_ref, ps_ref, a_ref, ubuf, q_new, kt_new, v_new):
    row_blocks = [slice(r, r + ROW_BLOCK) for r in range(0, TM, ROW_BLOCK)]
    hn = {}

    def norm(rows):
        hn[rows.start] = _rmsnorm(h_ref[rows, :], g_ref[...]).astype(_BF16)

    def proj_u(rows, cols):
        ubuf[POOL_HALO + rows.start:POOL_HALO + rows.stop, cols] = _dot(hn[rows.start], w_ref[:, cols])

    def proj_q(rows, cols):
        q = _dot(hn[rows.start], w_ref[:, POOL_W + cols.start:POOL_W + cols.stop]) * (1.0 / math.sqrt(HEAD_DIM))
        q_new[rows, cols] = q.astype(_BF16)

    def proj_k(rows, cols):
        kt_new[cols, rows] = _dot_t(wkt_ref[cols, :], hn[rows.start]).astype(_BF16)

    def proj_v(rows, cols):
        base = POOL_W + 2 * ATT_W
        v = _dot(hn[rows.start], w_ref[:, base + cols.start:base + cols.stop]).astype(_BF16)
        for i in range(COL_BLOCK // LANES):
            pair = cols.start // LANES + i
            v_new[rows, 2 * pair * LANES:(2 * pair + 1) * LANES] = v[:, i * LANES:(i + 1) * LANES]
            v_new[rows, (2 * pair + 1) * LANES:(2 * pair + 2) * LANES] = jnp.ones((ROW_BLOCK, LANES), _BF16)

    def pool(gi):
        win = POOL_WINDOWS[gi]
        cols = slice(gi * POOL_GROUP, (gi + 1) * POOL_GROUP)
        u = ubuf[POOL_HALO:POOL_HALO + TM, cols]
        s = u
        for d in range(1, win):
            s = s + ubuf[POOL_HALO - d:POOL_HALO - d + TM, cols]
        pos = tile * TM + lax.broadcasted_iota(jnp.int32, (TM, 1), 0)
        cnt = jnp.minimum(pos + 1, win).astype(_F32)
        y = (s / cnt - u).astype(_BF16)
        r = _dot(y, pw_ref[gi]) + pb_ref[:, cols]
        a_ref[:, cols] = (r * ps_ref[:, cols]).astype(_BF16)

    part = functools.partial
    col_blocks = [slice(c, c + COL_BLOCK) for c in range(0, ATT_W, COL_BLOCK)]
    norms = [part(norm, rows) for rows in row_blocks]
    projections = [part(proj, rows, cols) for proj in (proj_u, proj_q, proj_k, proj_v)
                   for rows in row_blocks for cols in col_blocks]
    pools = [part(pool, gi) for gi in range(len(POOL_WINDOWS))]
    return norms, projections, pools


def _build_bias_tables(rb_ref, tbl):
    row = lax.broadcasted_iota(jnp.int32, (QSUB, KWIN), 0)
    col = lax.broadcasted_iota(jnp.int32, (QSUB, KWIN), 1)
    first_key = row & ~(CHUNK - 1)
    visible = (col >= first_key) & (col < first_key + (LEFT_CHUNKS + 1) * CHUNK)
    brow = lax.broadcasted_iota(jnp.int32, (QSUB, BAND), 0)
    bcol = lax.broadcasted_iota(jnp.int32, (QSUB, BAND), 1)
    for head in range(N_HEADS):
        r = rb_ref[head]
        far = r[:, 0:1]
        rolled = pltpu.roll(jnp.broadcast_to(r, (QSUB, BAND)), 0, 1, stride=1, stride_axis=0)
        band = jnp.where(bcol >= brow, rolled, far)
        full = jnp.concatenate([jnp.broadcast_to(far, (QSUB, KWIN - BAND)), band], axis=1)
        tbl[head // 2, (head % 2) * QSUB:(head % 2 + 1) * QSUB, :] = jnp.where(visible, full, NEG)
    tbl[NEG_TABLE] = jnp.full((2 * QSUB, KWIN), NEG, _F32)


def _attention_pieces(has_prev, q_ref, ktp_ref, ktc_ref, vp_ref, vc_ref, tbl, s_s, m_s, o_ref):
    low_head = lax.broadcasted_iota(jnp.int32, (1, LANES), 1) < HEAD_DIM
    n_blocks = N_PAIRS * N_SUBS

    def geometry(n):
        pair, sub = divmod(n, N_SUBS)
        n_prev = TM - sub * QSUB
        return pair, slice(sub * QSUB, (sub + 1) * QSUB), slice(pair * LANES, (pair + 1) * LANES), n_prev

    def scores(n):
        pair, rows, cols, n_prev = geometry(n)
        q2 = q_ref[rows, cols]
        zero = jnp.zeros_like(q2)
        qq = jnp.concatenate([jnp.where(low_head, q2, zero), jnp.where(low_head, zero, q2)], axis=0)
        prev_tbl = jnp.where(has_prev, pair, NEG_TABLE)
        s_prev = _dot(qq, ktp_ref[cols, TM - n_prev:TM]) + tbl[prev_tbl, :, 0:n_prev]
        s_cur = _dot(qq, ktc_ref[cols, 0:KWIN - n_prev]) + tbl[pair, :, n_prev:KWIN]
        s_s[n % 2, :, 0:n_prev] = s_prev
        s_s[n % 2, :, n_prev:KWIN] = s_cur
        m = jnp.maximum(jnp.max(s_prev, axis=-1, keepdims=True), jnp.max(s_cur, axis=-1, keepdims=True))
        m_s[n % 2] = jnp.broadcast_to(m, (2 * QSUB, LANES))

    e = {}

    def weights(n):
        _, _, _, n_prev = geometry(n)
        m = m_s[n % 2]

        def exps(lo, hi):
            return jnp.concatenate([jnp.exp(s_s[n % 2, :, c:c + LANES] - m).astype(_BF16)
                                    for c in range(lo, hi, LANES)], axis=1)

        e[n] = (exps(0, n_prev), exps(n_prev, KWIN))

    def outputs(n):
        pair, rows, cols, n_prev = geometry(n)
        vcols = slice(2 * pair * LANES, (2 * pair + 2) * LANES)
        e_prev, e_cur = e.pop(n)
        ov = _dot(e_prev, vp_ref[TM - n_prev:TM, vcols]) + _dot(e_cur, vc_ref[0:KWIN - n_prev, vcols])
        o = ov[:, 0:LANES] / ov[:, LANES:2 * LANES]
        o_ref[rows, cols] = jnp.where(low_head, o[0:QSUB], o[QSUB:2 * QSUB]).astype(_BF16)

    part = functools.partial
    return [(part(scores, n), part(weights, n), part(outputs, n)) for n in range(n_blocks)]


def _mix_attn_kernel(tiles_per_seq, h_ref, g_ref, w_ref, wkt_ref, pw_ref, pb_ref, ps_ref, rb_ref,
                     a_ref, o_ref,
                     ubuf, q_cur, q_new, kt_prev, kt_cur, kt_new, v_prev, v_cur, v_new, tbl, s_s, m_s):
    step = pl.program_id(0)

    @pl.when(step == 0)
    def _():
        _build_bias_tables(rb_ref, tbl)
        ubuf[...] = jnp.zeros_like(ubuf)
        q_cur[...] = jnp.zeros_like(q_cur)
        kt_prev[...] = jnp.zeros_like(kt_prev)
        kt_cur[...] = jnp.zeros_like(kt_cur)
        v_prev[...] = jnp.ones_like(v_prev)
        v_cur[...] = jnp.ones_like(v_cur)

    tile = step % tiles_per_seq
    has_prev = ((step - 1) % tiles_per_seq) != 0
    ubuf[0:POOL_HALO, :] = jnp.where(tile == 0, 0.0, ubuf[TM:TM + POOL_HALO, :])

    norms, projections, pools = _mix_in_pieces(tile, h_ref, g_ref, w_ref, wkt_ref, pw_ref, pb_ref, ps_ref,
                                               a_ref, ubuf, q_new, kt_new, v_new)
    attn = _attention_pieces(has_prev, q_cur, kt_prev, kt_cur, v_prev, v_cur, tbl, s_s, m_s, o_ref)
    for run in norms:
        run()
    n_u = len(projections) // 4
    queue = projections[:n_u] + _interleave(projections[n_u:], pools)
    attn[0][0]()
    for n, (_, weights, outputs) in enumerate(attn):
        weights()
        if n + 1 < len(attn):
            attn[n + 1][0]()
        for run in queue[n * len(queue) // len(attn):(n + 1) * len(queue) // len(attn)]:
            run()
        outputs()

    @pl.when(step >= 0)
    def _():
        kt_prev[...] = kt_cur[...]
        kt_cur[...] = kt_new[...]
        v_prev[...] = v_cur[...]
        v_cur[...] = v_new[...]
        q_cur[...] = q_new[...]


def _mix_attn(h, layer, g, w_in, w_kt, pool_w, pool_b, pool_scale, rb, tiles_per_seq):
    t, d = h.shape
    n_tiles = t // TM
    kt_buf = pltpu.VMEM((ATT_W, TM), _BF16)
    v_buf = pltpu.VMEM((TM, 2 * ATT_W), _BF16)
    q_buf = pltpu.VMEM((TM, ATT_W), _BF16)
    return pl.pallas_call(
        functools.partial(_mix_attn_kernel, tiles_per_seq),
        grid=(n_tiles + 1,),
        in_specs=[
            pl.BlockSpec((TM, d), lambda j: (jnp.minimum(j, n_tiles - 1), 0)),
            _resident(g.shape, layer),
            _resident(w_in.shape, layer),
            _resident(w_kt.shape, layer),
            _resident(pool_w.shape, layer),
            _resident(pool_b.shape, layer),
            _resident(pool_scale.shape, layer),
            _resident(rb.shape, layer),
        ],
        out_specs=[_rows(POOL_W), pl.BlockSpec((TM, ATT_W), lambda j: (jnp.maximum(j - 1, 0), 0))],
        out_shape=[jax.ShapeDtypeStruct((t + TM, POOL_W), _BF16), jax.ShapeDtypeStruct((t, ATT_W), _BF16)],
        scratch_shapes=[
            pltpu.VMEM((POOL_HALO + TM, POOL_W), _F32),
            q_buf, q_buf,
            kt_buf, kt_buf, kt_buf,
            v_buf, v_buf, v_buf,
            pltpu.VMEM((N_PAIRS + 1, 2 * QSUB, KWIN), _F32),
            pltpu.VMEM((2, 2 * QSUB, KWIN), _F32),
            pltpu.VMEM((2, 2 * QSUB, LANES), _F32),
        ],
        compiler_params=pltpu.CompilerParams(
            dimension_semantics=("arbitrary",), vmem_limit_bytes=VMEM_LIMIT),
        name="mix_attn",
    )(h, g, w_in, w_kt, pool_w, pool_b, pool_scale, rb)


def _ff_chunks(d_ff):
    return tuple((off, min(FF_CHUNK, d_ff - off)) for off in range(0, d_ff, FF_CHUNK))


def _ffn_kernel(tiles_per_seq, d_ff, final, h_ref, a_ref, o_ref, p_ref, wout_ref, gf_ref, wup_ref, cw_ref,
                cb_ref, wdn_ref, gp_ref, wpg_ref, bpg_ref, wple_ref, fg_ref, out_ref,
                h1_s, hn_s, acc_s, gate_s, val_s, halo_s):
    @pl.when(pl.program_id(0) == 0)
    def _():
        halo_s[...] = jnp.zeros_like(halo_s)

    first = (pl.program_id(0) % tiles_per_seq) == 0
    chunks = _ff_chunks(d_ff)
    row_blocks = [slice(r, r + ROW_BLOCK) for r in range(0, TM, ROW_BLOCK)]

    def up(c, rows=slice(0, TM)):
        off, width = chunks[c]
        hn = hn_s[rows, :]
        gate_s[c % 2, CONV_HALO + rows.start:CONV_HALO + rows.stop, 0:width] = (
            _dot(hn, wup_ref[:, off:off + width]))
        val_s[c % 2, rows, 0:width] = _dot(hn, wup_ref[:, d_ff + off:d_ff + off + width])

    for rows in row_blocks:
        h1 = (h_ref[rows, :] + _dot(a_ref[rows, :], wout_ref[0:POOL_W, :])
              + _dot(o_ref[rows, :], wout_ref[POOL_W:POOL_W + ATT_W, :]))
        h1_s[rows, :] = h1
        hn_s[rows, :] = _rmsnorm(h1, gf_ref[...]).astype(_BF16)
    for rows in row_blocks:
        up(0, rows)

    def down(c):
        off, width = chunks[c]
        gate = gate_s.at[c % 2]
        gate[0:CONV_HALO, 0:width] = jnp.where(first, 0.0, halo_s[:, off:off + width])
        halo_s[:, off:off + width] = gate[TM:TM + CONV_HALO, 0:width]
        y = gate[CONV_HALO:CONV_HALO + TM, 0:width] * cw_ref[CONV_K - 1:CONV_K, off:off + width]
        y = y + cb_ref[:, off:off + width]
        for kk in range(CONV_K - 1):
            lag = CONV_K - 1 - kk
            y = y + gate[CONV_HALO - lag:CONV_HALO - lag + TM, 0:width] * cw_ref[kk:kk + 1, off:off + width]
        act = 0.5 * y * (1.0 + lax.erf(y * (1.0 / math.sqrt(2.0))))
        part = _dot((act * val_s[c % 2, :, 0:width]).astype(_BF16), wdn_ref[off:off + width, :])
        if c == 0:
            acc_s[...] = part
        else:
            acc_s[...] += part

    for c in range(len(chunks)):
        if c + 1 < len(chunks):
            up(c + 1)
        down(c)

    for rows in row_blocks:
        h2 = h1_s[rows, :] + acc_s[rows, :]
        hn3 = _rmsnorm(h2, gp_ref[...]).astype(_BF16)
        gate = jax.nn.sigmoid(_dot(hn3, wpg_ref[...]) + bpg_ref[...])
        h3 = h2 + gate * _dot(p_ref[rows, :].astype(_BF16), wple_ref[...])
        out_ref[rows, :] = _rmsnorm(h3, fg_ref[...]) if final else h3


def _ffn(h, a, o, layer, p, w_out, g_ffn, w_up, conv_w, conv_b, w_down, g_ple, w_pg, b_pg, w_ple, final_g,
         tiles_per_seq, final):
    t, d = h.shape
    d_ff = w_down.shape[1]
    per_layer = (w_out, g_ffn, w_up, conv_w, conv_b, w_down, g_ple, w_pg, b_pg, w_ple)
    return pl.pallas_call(
        functools.partial(_ffn_kernel, tiles_per_seq, d_ff, final),
        grid=(t // TM,),
        in_specs=[_rows(d), _rows(POOL_W), _rows(ATT_W),
                  pl.BlockSpec((None, TM, p.shape[2]), lambda j: (layer, j, 0))]
        + [_resident(w.shape, layer) for w in per_layer] + [_resident(final_g.shape)],
        out_specs=_rows(d),
        out_shape=jax.ShapeDtypeStruct((t, d), _F32),
        scratch_shapes=[
            pltpu.VMEM((TM, d), _F32),
            pltpu.VMEM((TM, d), _BF16),
            pltpu.VMEM((TM, d), _F32),
            pltpu.VMEM((2, CONV_HALO + TM, FF_CHUNK), _F32),
            pltpu.VMEM((2, TM, FF_CHUNK), _F32),
            pltpu.VMEM((CONV_HALO, d_ff), _F32),
        ],
        compiler_params=pltpu.CompilerParams(
            dimension_semantics=("arbitrary",), vmem_limit_bytes=VMEM_LIMIT),
        name="ffn",
    )(h, a, o, p, *per_layer, final_g)


def kernel(x, p, norm_mix_g, w_in, pool_w, pool_b, pool_scale, rel_bias, w_out, norm_ffn_g, w_up,
           conv_w, conv_b, w_down, norm_ple_g, w_ple_gate, b_ple_gate, w_ple, final_g):
    b, s, d = x.shape
    depth = w_in.shape[0]
    d_ff = w_down.shape[1]
    assert s % TM == 0 and d_ff % LANES == 0 and w_up.shape[2] == 2 * d_ff
    assert w_in.shape[2] == POOL_W + 3 * ATT_W and w_out.shape[1] == POOL_W + ATT_W
    assert rel_bias.shape[1:] == (N_HEADS, 2 * MAX_REL + 1)
    t = b * s
    tiles_per_seq = s // TM

    w_in_b = w_in.astype(_BF16)
    w_k = lax.optimization_barrier(w_in[:, :, POOL_W + ATT_W:POOL_W + 2 * ATT_W])
    w_kt_b = w_k.transpose(0, 2, 1).astype(_BF16)
    pool_w_b = pool_w.astype(_BF16)
    w_out_b = w_out.astype(_BF16)
    w_up_b = w_up.astype(_BF16)
    w_down_b = w_down.astype(_BF16)
    w_pg_b = w_ple_gate.astype(_BF16)
    w_ple_b = w_ple.astype(_BF16)
    rb = rel_bias.astype(_F32)[:, :, :0:-1].reshape(depth, N_HEADS, 1, BAND)
    rows = lambda v: v.reshape(depth, 1, -1)
    p_rows = p.reshape(depth, t, -1)

    h = x.reshape(t, d)
    for i in range(depth):
        a, o = _mix_attn(h, i, rows(norm_mix_g), w_in_b, w_kt_b, pool_w_b, rows(pool_b), rows(pool_scale), rb,
                         tiles_per_seq)
        h = _ffn(h, a, o, i, p_rows, w_out_b, rows(norm_ffn_g), w_up_b, conv_w, rows(conv_b), w_down_b,
                 rows(norm_ple_g), w_pg_b, rows(b_ple_gate), w_ple_b, final_g.reshape(1, -1), tiles_per_seq,
                 final=(i == depth - 1))
    return h.reshape(b, s, d)
```

```python
import functools
import math

import jax
import jax.numpy as jnp
from jax import lax
from jax.experimental import pallas as pl
from jax.experimental.pallas import tpu as pltpu

CHUNK = 64
POOL_WINDOWS = (2, 4, 8, 16)
POOL_GROUP = 128
POOL_W = POOL_GROUP * len(POOL_WINDOWS)
N_HEADS = 8
HEAD_DIM = 64
ATT_W = N_HEADS * HEAD_DIM
LEFT_CHUNKS = 8
LEFT = LEFT_CHUNKS * CHUNK
MAX_REL = 128
CONV_K = 3
EPS = 1e-6

LANES = 128
SUBLANES = 8
TM = LEFT
QSUB = 2 * CHUNK
KWIN = QSUB + LEFT
N_PAIRS = N_HEADS * HEAD_DIM // LANES
N_SUBS = TM // QSUB
BAND = 2 * MAX_REL
NEG_TABLE = N_PAIRS
ROW_BLOCK = 256
COL_BLOCK = 256
POOL_HALO = 16
CONV_HALO = SUBLANES
FF_CHUNK = 768
NEG = -1e30
VMEM_LIMIT = 56 * 1024 * 1024

_BF16 = jnp.bfloat16
_F32 = jnp.float32


def _dot(a, b):
    return jnp.dot(a, b, preferred_element_type=_F32)


def _dot_t(a, b):
    return lax.dot_general(a, b, (((1,), (1,)), ((), ())), preferred_element_type=_F32)


def _rmsnorm(x, g):
    ms = jnp.mean(x * x, axis=-1, keepdims=True)
    return x * lax.rsqrt(ms + EPS) * g


def _resident(shape, layer=None):
    if layer is None:
        return pl.BlockSpec(shape, lambda j: (0,) * len(shape), pipeline_mode=pl.Buffered(1))
    return pl.BlockSpec((None,) + tuple(shape[1:]), lambda j: (layer,) + (0,) * (len(shape) - 1),
                        pipeline_mode=pl.Buffered(1))


def _rows(width):
    return pl.BlockSpec((TM, width), lambda j: (j, 0))


def _interleave(xs, ys):
    out, taken = [], 0
    for i, x in enumerate(xs):
        out.append(x)
        want = (i + 1) * len(ys) // len(xs)
        out += ys[taken:want]
        taken = want
    return out


def _mix_in_pieces(tile, h_ref, g_ref, w_ref, wkt_ref, pw_ref, pb_ref, ps_ref, a_ref, ubuf, q_new, kt_new, v_new):
    row_blocks = [slice(r, r + ROW_BLOCK) for r in range(0, TM, ROW_BLOCK)]
    hn = {}

    def norm(rows):
        hn[rows.start] = _rmsnorm(h_ref[rows, :], g_ref[...]).astype(_BF16)

    def proj_u(rows, cols):
        ubuf[POOL_HALO + rows.start:POOL_HALO + rows.stop, cols] = _dot(hn[rows.start], w_ref[:, cols])

    def proj_q(rows, cols):
        q = _dot(hn[rows.start], w_ref[:, POOL_W + cols.start:POOL_W + cols.stop]) * (1.0 / math.sqrt(HEAD_DIM))
        q_new[rows, cols] = q.astype(_BF16)

    def proj_k(rows, cols):
        kt_new[cols, rows] = _dot_t(wkt_ref[cols, :], hn[rows.start]).astype(_BF16)

    def proj_v(rows, cols):
        base = POOL_W + 2 * ATT_W
        v = _dot(hn[rows.start], w_ref[:, base + cols.start:base + cols.stop]).astype(_BF16)
        for i in range(COL_BLOCK // LANES):
            pair = cols.start // LANES + i
            v_new[rows, 2 * pair * LANES:(2 * pair + 1) * LANES] = v[:, i * LANES:(i + 1) * LANES]
            v_new[rows, (2 * pair + 1) * LANES:(2 * pair + 2) * LANES] = jnp.ones((ROW_BLOCK, LANES), _BF16)

    def pool(gi):
        win = POOL_WINDOWS[gi]
        cols = slice(gi * POOL_GROUP, (gi + 1) * POOL_GROUP)
        u = ubuf[POOL_HALO:POOL_HALO + TM, cols]
        s = u
        for d in range(1, win):
            s = s + ubuf[POOL_HALO - d:POOL_HALO - d + TM, cols]
        pos = tile * TM + lax.broadcasted_iota(jnp.int32, (TM, 1), 0)
        cnt = jnp.minimum(pos + 1, win).astype(_F32)
        y = (s / cnt - u).astype(_BF16)
        r = _dot(y, pw_ref[gi]) + pb_ref[:, cols]
        a_ref[:, cols] = (r * ps_ref[:, cols]).astype(_BF16)

    part = functools.partial
    col_blocks = [slice(c, c + COL_BLOCK) for c in range(0, ATT_W, COL_BLOCK)]
    norms = [part(norm, rows) for rows in row_blocks]
    projections = [part(proj, rows, cols) for proj in (proj_u, proj_q, proj_k, proj_v)
                   for rows in row_blocks for cols in col_blocks]
    pools = [part(pool, gi) for gi in range(len(POOL_WINDOWS))]
    return norms, projections, pools


def _build_bias_tables(rb_ref, tbl):
    row = lax.broadcasted_iota(jnp.int32, (QSUB, KWIN), 0)
    col = lax.broadcasted_iota(jnp.int32, (QSUB, KWIN), 1)
    first_key = row & ~(CHUNK - 1)
    visible = (col >= first_key) & (col < first_key + (LEFT_CHUNKS + 1) * CHUNK)
    brow = lax.broadcasted_iota(jnp.int32, (QSUB, BAND), 0)
    bcol = lax.broadcasted_iota(jnp.int32, (QSUB, BAND), 1)
    for head in range(N_HEADS):
        r = rb_ref[head]
        far = r[:, 0:1]
        rolled = pltpu.roll(jnp.broadcast_to(r, (QSUB, BAND)), 0, 1, stride=1, stride_axis=0)
        band = jnp.where(bcol >= brow, rolled, far)
        full = jnp.concatenate([jnp.broadcast_to(far, (QSUB, KWIN - BAND)), band], axis=1)
        tbl[head // 2, (head % 2) * QSUB:(head % 2 + 1) * QSUB, :] = jnp.where(visible, full, NEG)
    tbl[NEG_TABLE] = jnp.full((2 * QSUB, KWIN), NEG, _F32)


def _attention_pieces(has_prev, q_ref, ktp_ref, ktc_ref, vp_ref, vc_ref, tbl, s_s, m_s, o_ref):
    low_head = lax.broadcasted_iota(jnp.int32, (1, LANES), 1) < HEAD_DIM
    n_blocks = N_PAIRS * N_SUBS

    def geometry(n):
        pair, sub = divmod(n, N_SUBS)
        n_prev = TM - sub * QSUB
        return pair, slice(sub * QSUB, (sub + 1) * QSUB), slice(pair * LANES, (pair + 1) * LANES), n_prev

    def scores(n):
        pair, rows, cols, n_prev = geometry(n)
        q2 = q_ref[rows, cols]
        zero = jnp.zeros_like(q2)
        qq = jnp.concatenate([jnp.where(low_head, q2, zero), jnp.where(low_head, zero, q2)], axis=0)
        prev_tbl = jnp.where(has_prev, pair, NEG_TABLE)
        s_prev = _dot(qq, ktp_ref[cols, TM - n_prev:TM]) + tbl[prev_tbl, :, 0:n_prev]
        s_cur = _dot(qq, ktc_ref[cols, 0:KWIN - n_prev]) + tbl[pair, :, n_prev:KWIN]
        s_s[n % 2, :, 0:n_prev] = s_prev
        s_s[n % 2, :, n_prev:KWIN] = s_cur
        m = jnp.maximum(jnp.max(s_prev, axis=-1, keepdims=True), jnp.max(s_cur, axis=-1, keepdims=True))
        m_s[n % 2] = jnp.broadcast_to(m, (2 * QSUB, LANES))

    e = {}

    def weights(n):
        _, _, _, n_prev = geometry(n)
        m = m_s[n % 2]

        def exps(lo, hi):
            return jnp.concatenate([jnp.exp(s_s[n % 2, :, c:c + LANES] - m).astype(_BF16)
                                    for c in range(lo, hi, LANES)], axis=1)

        e[n] = (exps(0, n_prev), exps(n_prev, KWIN))

    def outputs(n):
        pair, rows, cols, n_prev = geometry(n)
        vcols = slice(2 * pair * LANES, (2 * pair + 2) * LANES)
        e_prev, e_cur = e.pop(n)
        ov = _dot(e_prev, vp_ref[TM - n_prev:TM, vcols]) + _dot(e_cur, vc_ref[0:KWIN - n_prev, vcols])
        o = ov[:, 0:LANES] / ov[:, LANES:2 * LANES]
        o_ref[rows, cols] = jnp.where(low_head, o[0:QSUB], o[QSUB:2 * QSUB]).astype(_BF16)

    part = functools.partial
    return [(part(scores, n), part(weights, n), part(outputs, n)) for n in range(n_blocks)]


def _mix_attn_kernel(tiles_per_seq, h_ref, g_ref, w_ref, wkt_ref, pw_ref, pb_ref, ps_ref, rb_ref,
                     a_ref, o_ref,
                     ubuf, q_cur, q_new, kt_prev, kt_cur, kt_new, v_prev, v_cur, v_new, tbl, s_s, m_s):
    step = pl.program_id(0)

    @pl.when(step == 0)
    def _():
        _build_bias_tables(rb_ref, tbl)
        ubuf[...] = jnp.zeros_like(ubuf)
        q_new[...] = jnp.zeros_like(q_new)
        kt_cur[...] = jnp.zeros_like(kt_cur)
        kt_new[...] = jnp.zeros_like(kt_new)
        v_cur[...] = jnp.ones_like(v_cur)
        v_new[...] = jnp.ones_like(v_new)

    kt_prev[...] = kt_cur[...]
    kt_cur[...] = kt_new[...]
    v_prev[...] = v_cur[...]
    v_cur[...] = v_new[...]
    q_cur[...] = q_new[...]

    tile = step % tiles_per_seq
    has_prev = ((step - 1) % tiles_per_seq) != 0
    ubuf[0:POOL_HALO, :] = jnp.where(tile == 0, 0.0, ubuf[TM:TM + POOL_HALO, :])

    norms, projections, pools = _mix_in_pieces(tile, h_ref, g_ref, w_ref, wkt_ref, pw_ref, pb_ref, ps_ref,
                                               a_ref, ubuf, q_new, kt_new, v_new)
    attn = _attention_pieces(has_prev, q_cur, kt_prev, kt_cur, v_prev, v_cur, tbl, s_s, m_s, o_ref)
    for run in norms:
        run()
    n_u = len(projections) // 4
    queue = projections[:n_u] + _interleave(projections[n_u:], pools)
    attn[0][0]()
    for n, (_, weights, outputs) in enumerate(attn):
        weights()
        if n + 1 < len(attn):
            attn[n + 1][0]()
        for run in queue[n * len(queue) // len(attn):(n + 1) * len(queue) // len(attn)]:
            run()
        outputs()


def _mix_attn(h, layer, g, w_in, w_kt, pool_w, pool_b, pool_scale, rb, tiles_per_seq):
    t, d = h.shape
    n_tiles = t // TM
    kt_buf = pltpu.VMEM((ATT_W, TM), _BF16)
    v_buf = pltpu.VMEM((TM, 2 * ATT_W), _BF16)
    q_buf = pltpu.VMEM((TM, ATT_W), _BF16)
    return pl.pallas_call(
        functools.partial(_mix_attn_kernel, tiles_per_seq),
        grid=(n_tiles + 1,),
        in_specs=[
            pl.BlockSpec((TM, d), lambda j: (jnp.minimum(j, n_tiles - 1), 0)),
            _resident(g.shape, layer),
            _resident(w_in.shape, layer),
            _resident(w_kt.shape, layer),
            _resident(pool_w.shape, layer),
            _resident(pool_b.shape, layer),
            _resident(pool_scale.shape, layer),
            _resident(rb.shape, layer),
        ],
        out_specs=[_rows(POOL_W), pl.BlockSpec((TM, ATT_W), lambda j: (jnp.maximum(j - 1, 0), 0))],
        out_shape=[jax.ShapeDtypeStruct((t + TM, POOL_W), _BF16), jax.ShapeDtypeStruct((t, ATT_W), _BF16)],
        scratch_shapes=[
            pltpu.VMEM((POOL_HALO + TM, POOL_W), _F32),
            q_buf, q_buf,
            kt_buf, kt_buf, kt_buf,
            v_buf, v_buf, v_buf,
            pltpu.VMEM((N_PAIRS + 1, 2 * QSUB, KWIN), _F32),
            pltpu.VMEM((2, 2 * QSUB, KWIN), _F32),
            pltpu.VMEM((2, 2 * QSUB, LANES), _F32),
        ],
        compiler_params=pltpu.CompilerParams(
            dimension_semantics=("arbitrary",), vmem_limit_bytes=VMEM_LIMIT),
        name="mix_attn",
    )(h, g, w_in, w_kt, pool_w, pool_b, pool_scale, rb)


def _ff_chunks(d_ff):
    return tuple((off, min(FF_CHUNK, d_ff - off)) for off in range(0, d_ff, FF_CHUNK))


def _ffn_kernel(tiles_per_seq, d_ff, final, h_ref, a_ref, o_ref, p_ref, wout_ref, gf_ref, wup_ref, cw_ref,
                cb_ref, wdn_ref, gp_ref, wpg_ref, bpg_ref, wple_ref, fg_ref, out_ref,
                h1_s, hn_s, acc_s, gate_s, val_s, halo_s):
    @pl.when(pl.program_id(0) == 0)
    def _():
        halo_s[...] = jnp.zeros_like(halo_s)

    first = (pl.program_id(0) % tiles_per_seq) == 0
    chunks = _ff_chunks(d_ff)
    row_blocks = [slice(r, r + ROW_BLOCK) for r in range(0, TM, ROW_BLOCK)]

    def up(c, rows=slice(0, TM)):
        off, width = chunks[c]
        hn = hn_s[rows, :]
        gate_s[c % 2, CONV_HALO + rows.start:CONV_HALO + rows.stop, 0:width] = (
            _dot(hn, wup_ref[:, off:off + width]))
        val_s[c % 2, rows, 0:width] = _dot(hn, wup_ref[:, d_ff + off:d_ff + off + width])

    for rows in row_blocks:
        h1 = (h_ref[rows, :] + _dot(a_ref[rows, :], wout_ref[0:POOL_W, :])
              + _dot(o_ref[rows, :], wout_ref[POOL_W:POOL_W + ATT_W, :]))
        h1_s[rows, :] = h1
        hn_s[rows, :] = _rmsnorm(h1, gf_ref[...]).astype(_BF16)
    for rows in row_blocks:
        up(0, rows)

    def down(c):
        off, width = chunks[c]
        gate = gate_s.at[c % 2]
        gate[0:CONV_HALO, 0:width] = jnp.where(first, 0.0, halo_s[:, off:off + width])
        halo_s[:, off:off + width] = gate[TM:TM + CONV_HALO, 0:width]
        y = gate[CONV_HALO:CONV_HALO + TM, 0:width] * cw_ref[CONV_K - 1:CONV_K, off:off + width]
        y = y + cb_ref[:, off:off + width]
        for kk in range(CONV_K - 1):
            lag = CONV_K - 1 - kk
            y = y + gate[CONV_HALO - lag:CONV_HALO - lag + TM, 0:width] * cw_ref[kk:kk + 1, off:off + width]
        act = 0.5 * y * (1.0 + lax.erf(y * (1.0 / math.sqrt(2.0))))
        part = _dot((act * val_s[c % 2, :, 0:width]).astype(_BF16), wdn_ref[off:off + width, :])
        if c == 0:
            acc_s[...] = part
        else:
            acc_s[...] += part

    for c in range(len(chunks)):
        if c + 1 < len(chunks):
            up(c + 1)
        down(c)

    for rows in row_blocks:
        h2 = h1_s[rows, :] + acc_s[rows, :]
        hn3 = _rmsnorm(h2, gp_ref[...]).astype(_BF16)
        gate = jax.nn.sigmoid(_dot(hn3, wpg_ref[...]) + bpg_ref[...])
        h3 = h2 + gate * _dot(p_ref[rows, :].astype(_BF16), wple_ref[...])
        out_ref[rows, :] = _rmsnorm(h3, fg_ref[...]) if final else h3


def _ffn(h, a, o, layer, p, w_out, g_ffn, w_up, conv_w, conv_b, w_down, g_ple, w_pg, b_pg, w_ple, final_g,
         tiles_per_seq, final):
    t, d = h.shape
    d_ff = w_down.shape[1]
    per_layer = (w_out, g_ffn, w_up, conv_w, conv_b, w_down, g_ple, w_pg, b_pg, w_ple)
    return pl.pallas_call(
        functools.partial(_ffn_kernel, tiles_per_seq, d_ff, final),
        grid=(t // TM,),
        in_specs=[_rows(d), _rows(POOL_W), _rows(ATT_W),
                  pl.BlockSpec((None, TM, p.shape[2]), lambda j: (layer, j, 0))]
        + [_resident(w.shape, layer) for w in per_layer] + [_resident(final_g.shape)],
        out_specs=_rows(d),
        out_shape=jax.ShapeDtypeStruct((t, d), _F32),
        scratch_shapes=[
            pltpu.VMEM((TM, d), _F32),
            pltpu.VMEM((TM, d), _BF16),
            pltpu.VMEM((TM, d), _F32),
            pltpu.VMEM((2, CONV_HALO + TM, FF_CHUNK), _F32),
            pltpu.VMEM((2, TM, FF_CHUNK), _F32),
            pltpu.VMEM((CONV_HALO, d_ff), _F32),
        ],
        compiler_params=pltpu.CompilerParams(
            dimension_semantics=("arbitrary",), vmem_limit_bytes=VMEM_LIMIT),
        name="ffn",
    )(h, a, o, p, *per_layer, final_g)


def kernel(x, p, norm_mix_g, w_in, pool_w, pool_b, pool_scale, rel_bias, w_out, norm_ffn_g, w_up,
           conv_w, conv_b, w_down, norm_ple_g, w_ple_gate, b_ple_gate, w_ple, final_g):
    b, s, d = x.shape
    depth = w_in.shape[0]
    d_ff = w_down.shape[1]
    assert s % TM == 0 and d_ff % LANES == 0 and w_up.shape[2] == 2 * d_ff
    assert w_in.shape[2] == POOL_W + 3 * ATT_W and w_out.shape[1] == POOL_W + ATT_W
    assert rel_bias.shape[1:] == (N_HEADS, 2 * MAX_REL + 1)
    t = b * s
    tiles_per_seq = s // TM

    w_in_b = w_in.astype(_BF16)
    w_k = lax.optimization_barrier(w_in[:, :, POOL_W + ATT_W:POOL_W + 2 * ATT_W])
    w_kt_b = w_k.transpose(0, 2, 1).astype(_BF16)
    pool_w_b = pool_w.astype(_BF16)
    w_out_b = w_out.astype(_BF16)
    w_up_b = w_up.astype(_BF16)
    w_down_b = w_down.astype(_BF16)
    w_pg_b = w_ple_gate.astype(_BF16)
    w_ple_b = w_ple.astype(_BF16)
    rb = rel_bias.astype(_F32)[:, :, :0:-1].reshape(depth, N_HEADS, 1, BAND)
    rows = lambda v: v.reshape(depth, 1, -1)
    p_rows = p.reshape(depth, t, -1)

    h = x.reshape(t, d)
    for i in range(depth):
        a, o = _mix_attn(h, i, rows(norm_mix_g), w_in_b, w_kt_b, pool_w_b, rows(pool_b), rows(pool_scale), rb,
                         tiles_per_seq)
        h = _ffn(h, a, o, i, p_rows, w_out_b, rows(norm_ffn_g), w_up_b, conv_w, rows(conv_b), w_down_b,
                 rows(norm_ple_g), w_pg_b, rows(b_ple_gate), w_ple_b, final_g.reshape(1, -1), tiles_per_seq,
                 final=(i == depth - 1))
    return h.reshape(b, s, d)
```

```python
import functools
import math

import jax
import jax.numpy as jnp
from jax import lax
from jax.experimental import pallas as pl
from jax.experimental.pallas import tpu as pltpu

CHUNK = 64
POOL_WINDOWS = (2, 4, 8, 16)
POOL_GROUP = 128
POOL_W = POOL_GROUP * len(POOL_WINDOWS)
N_HEADS = 8
HEAD_DIM = 64
ATT_W = N_HEADS * HEAD_DIM
LEFT_CHUNKS = 8
LEFT = LEFT_CHUNKS * CHUNK
MAX_REL = 128
CONV_K = 3
EPS = 1e-6

LANES = 128
SUBLANES = 8
TM = LEFT
QSUB = 2 * CHUNK
KWIN = QSUB + LEFT
N_PAIRS = N_HEADS * HEAD_DIM // LANES
N_SUBS = TM // QSUB
BAND = 2 * MAX_REL
NEG_TABLE = N_PAIRS
ROW_BLOCK = 256
COL_BLOCK = 256
POOL_HALO = 16
CONV_HALO = SUBLANES
FF_CHUNK = 768
NEG = -1e30
VMEM_LIMIT = 56 * 1024 * 1024

_BF16 = jnp.bfloat16
_F32 = jnp.float32


def _dot(a, b):
    return jnp.dot(a, b, preferred_element_type=_F32)


def _dot_t(a, b):
    return lax.dot_general(a, b, (((1,), (1,)), ((), ())), preferred_element_type=_F32)


def _rmsnorm(x, g):
    ms = jnp.mean(x * x, axis=-1, keepdims=True)
    return x * lax.rsqrt(ms + EPS) * g


def _resident(shape, layer=None):
    if layer is None:
        return pl.BlockSpec(shape, lambda j: (0,) * len(shape), pipeline_mode=pl.Buffered(1))
    return pl.BlockSpec((None,) + tuple(shape[1:]), lambda j: (layer,) + (0,) * (len(shape) - 1),
                        pipeline_mode=pl.Buffered(1))


def _rows(width):
    return pl.BlockSpec((TM, width), lambda j: (j, 0))


def _interleave(xs, ys):
    out, taken = [], 0
    for i, x in enumerate(xs):
        out.append(x)
        want = (i + 1) * len(ys) // len(xs)
        out += ys[taken:want]
        taken = want
    return out


def _mix_in_pieces(tile, h_ref, g_ref, w_ref, wkt_ref, pw_ref, pb_ref, ps_ref, a_ref, ubuf, q_new, kt_new, v_new):
    row_blocks = [slice(r, r + ROW_BLOCK) for r in range(0, TM, ROW_BLOCK)]
    hn = {}

    def norm(rows):
        hn[rows.start] = _rmsnorm(h_ref[rows, :], g_ref[...]).astype(_BF16)

    def proj_u(rows, cols):
        ubuf[POOL_HALO + rows.start:POOL_HALO + rows.stop, cols] = _dot(hn[rows.start], w_ref[:, cols])

    def proj_q(rows, cols):
        q = _dot(hn[rows.start], w_ref[:, POOL_W + cols.start:POOL_W + cols.stop]) * (1.0 / math.sqrt(HEAD_DIM))
        q_new[rows, cols] = q.astype(_BF16)

    def proj_k(rows, cols):
        kt_new[cols, rows] = _dot_t(wkt_ref[cols, :], hn[rows.start]).astype(_BF16)

    def proj_v(rows, cols):
        base = POOL_W + 2 * ATT_W
        v = _dot(hn[rows.start], w_ref[:, base + cols.start:base + cols.stop]).astype(_BF16)
        for i in range(COL_BLOCK // LANES):
            pair = cols.start // LANES + i
            v_new[rows, 2 * pair * LANES:(2 * pair + 1) * LANES] = v[:, i * LANES:(i + 1) * LANES]
            v_new[rows, (2 * pair + 1) * LANES:(2 * pair + 2) * LANES] = jnp.ones((ROW_BLOCK, LANES), _BF16)

    def pool(gi):
        win = POOL_WINDOWS[gi]
        cols = slice(gi * POOL_GROUP, (gi + 1) * POOL_GROUP)
        u = ubuf[POOL_HALO:POOL_HALO + TM, cols]
        s = u
        for d in range(1, win):
            s = s + ubuf[POOL_HALO - d:POOL_HALO - d + TM, cols]
        pos = tile * TM + lax.broadcasted_iota(jnp.int32, (TM, 1), 0)
        cnt = jnp.minimum(pos + 1, win).astype(_F32)
        y = (s / cnt - u).astype(_BF16)
        r = _dot(y, pw_ref[gi]) + pb_ref[:, cols]
        a_ref[:, cols] = (r * ps_ref[:, cols]).astype(_BF16)

    part = functools.partial
    col_blocks = [slice(c, c + COL_BLOCK) for c in range(0, ATT_W, COL_BLOCK)]
    norms = [part(norm, rows) for rows in row_blocks]
    projections = [part(proj, rows, cols) for proj in (proj_u, proj_q, proj_k, proj_v)
                   for rows in row_blocks for cols in col_blocks]
    pools = [part(pool, gi) for gi in range(len(POOL_WINDOWS))]
    return norms, projections, pools


def _build_bias_tables(rb_ref, tbl):
    row = lax.broadcasted_iota(jnp.int32, (QSUB, KWIN), 0)
    col = lax.broadcasted_iota(jnp.int32, (QSUB, KWIN), 1)
    first_key = row & ~(CHUNK - 1)
    visible = (col >= first_key) & (col < first_key + (LEFT_CHUNKS + 1) * CHUNK)
    brow = lax.broadcasted_iota(jnp.int32, (QSUB, BAND), 0)
    bcol = lax.broadcasted_iota(jnp.int32, (QSUB, BAND), 1)
    for head in range(N_HEADS):
        r = rb_ref[head]
        far = r[:, 0:1]
        rolled = pltpu.roll(jnp.broadcast_to(r, (QSUB, BAND)), 0, 1, stride=1, stride_axis=0)
        band = jnp.where(bcol >= brow, rolled, far)
        full = jnp.concatenate([jnp.broadcast_to(far, (QSUB, KWIN - BAND)), band], axis=1)
        tbl[head // 2, (head % 2) * QSUB:(head % 2 + 1) * QSUB, :] = jnp.where(visible, full, NEG)
    tbl[NEG_TABLE] = jnp.full((2 * QSUB, KWIN), NEG, _F32)


def _attention_pieces(has_prev, q_ref, ktp_ref, ktc_ref, vp_ref, vc_ref, tbl, s_s, m_s, o_ref):
    low_head = lax.broadcasted_iota(jnp.int32, (1, LANES), 1) < HEAD_DIM
    n_blocks = N_PAIRS * N_SUBS

    def geometry(n):
        pair, sub = divmod(n, N_SUBS)
        n_prev = TM - sub * QSUB
        return pair, slice(sub * QSUB, (sub + 1) * QSUB), slice(pair * LANES, (pair + 1) * LANES), n_prev

    def scores(n):
        pair, rows, cols, n_prev = geometry(n)
        q2 = q_ref[rows, cols]
        zero = jnp.zeros_like(q2)
        qq = jnp.concatenate([jnp.where(low_head, q2, zero), jnp.where(low_head, zero, q2)], axis=0)
        prev_tbl = jnp.where(has_prev, pair, NEG_TABLE)
        s_prev = _dot(qq, ktp_ref[cols, TM - n_prev:TM]) + tbl[prev_tbl, :, 0:n_prev]
        s_cur = _dot(qq, ktc_ref[cols, 0:KWIN - n_prev]) + tbl[pair, :, n_prev:KWIN]
        s_s[n % 2, :, 0:n_prev] = s_prev
        s_s[n % 2, :, n_prev:KWIN] = s_cur
        m = jnp.maximum(jnp.max(s_prev, axis=-1, keepdims=True), jnp.max(s_cur, axis=-1, keepdims=True))
        m_s[n % 2] = jnp.broadcast_to(m, (2 * QSUB, LANES))

    e = {}

    def weights(n):
        _, _, _, n_prev = geometry(n)
        m = m_s[n % 2]

        def exps(lo, hi):
            return jnp.concatenate([jnp.exp(s_s[n % 2, :, c:c + LANES] - m).astype(_BF16)
                                    for c in range(lo, hi, LANES)], axis=1)

        e[n] = (exps(0, n_prev), exps(n_prev, KWIN))

    def outputs(n):
        pair, rows, cols, n_prev = geometry(n)
        vcols = slice(2 * pair * LANES, (2 * pair + 2) * LANES)
        e_prev, e_cur = e.pop(n)
        ov = _dot(e_prev, vp_ref[TM - n_prev:TM, vcols]) + _dot(e_cur, vc_ref[0:KWIN - n_prev, vcols])
        o = ov[:, 0:LANES] / ov[:, LANES:2 * LANES]
        o_ref[rows, cols] = jnp.where(low_head, o[0:QSUB], o[QSUB:2 * QSUB]).astype(_BF16)

    part = functools.partial
    return [(part(scores, n), part(weights, n), part(outputs, n)) for n in range(n_blocks)]


def _mix_attn_kernel(tiles_per_seq, h_ref, g_ref, w_ref, wkt_ref, pw_ref, pb_ref, ps_ref, rb_ref,
                     a_ref, o_ref,
                     ubuf, q_cur, q_new, kt_prev, kt_cur, kt_new, v_prev, v_cur, v_new, tbl, s_s, m_s):
    step = pl.program_id(0)

    @pl.when(step == 0)
    def _():
        _build_bias_tables(rb_ref, tbl)
        ubuf[...] = jnp.zeros_like(ubuf)
        q_new[...] = jnp.zeros_like(q_new)
        kt_cur[...] = jnp.zeros_like(kt_cur)
        kt_new[...] = jnp.zeros_like(kt_new)
        v_cur[...] = jnp.ones_like(v_cur)
        v_new[...] = jnp.ones_like(v_new)

    kt_prev[...] = kt_cur[...]
    kt_cur[...] = kt_new[...]
    v_prev[...] = v_cur[...]
    v_cur[...] = v_new[...]
    q_cur[...] = q_new[...]

    tile = step % tiles_per_seq
    has_prev = ((step - 1) % tiles_per_seq) != 0
    ubuf[0:POOL_HALO, :] = jnp.where(tile == 0, 0.0, ubuf[TM:TM + POOL_HALO, :])

    norms, projections, pools = _mix_in_pieces(tile, h_ref, g_ref, w_ref, wkt_ref, pw_ref, pb_ref, ps_ref,
                                               a_ref, ubuf, q_new, kt_new, v_new)
    attn = _attention_pieces(has_prev, q_cur, kt_prev, kt_cur, v_prev, v_cur, tbl, s_s, m_s, o_ref)
    for run in norms:
        run()
    n_u = len(projections) // 4
    queue = projections[:n_u] + _interleave(projections[n_u:], pools)
    attn[0][0]()
    for n, (_, weights, outputs) in enumerate(attn):
        weights()
        if n + 1 < len(attn):
            attn[n + 1][0]()
        for run in queue[n * len(queue) // len(attn):(n + 1) * len(queue) // len(attn)]:
            run()
        outputs()


def _mix_attn(h, layer, g, w_in, w_kt, pool_w, pool_b, pool_scale, rb, tiles_per_seq):
    t, d = h.shape
    n_tiles = t // TM
    kt_buf = pltpu.VMEM((ATT_W, TM), _BF16)
    v_buf = pltpu.VMEM((TM, 2 * ATT_W), _BF16)
    q_buf = pltpu.VMEM((TM, ATT_W), _BF16)
    return pl.pallas_call(
        functools.partial(_mix_attn_kernel, tiles_per_seq),
        grid=(n_tiles + 1,),
        in_specs=[
            pl.BlockSpec((TM, d), lambda j: (jnp.minimum(j, n_tiles - 1), 0)),
            _resident(g.shape, layer),
            _resident(w_in.shape, layer),
            _resident(w_kt.shape, layer),
            _resident(pool_w.shape, layer),
            _resident(pool_b.shape, layer),
            _resident(pool_scale.shape, layer),
            _resident(rb.shape, layer),
        ],
        out_specs=[_rows(POOL_W), pl.BlockSpec((TM, ATT_W), lambda j: (jnp.maximum(j - 1, 0), 0))],
        out_shape=[jax.ShapeDtypeStruct((t + TM, POOL_W), _BF16), jax.ShapeDtypeStruct((t, ATT_W), _BF16)],
        scratch_shapes=[
            pltpu.VMEM((POOL_HALO + TM, POOL_W), _F32),
            q_buf, q_buf,
            kt_buf, kt_buf, kt_buf,
            v_buf, v_buf, v_buf,
            pltpu.VMEM((N_PAIRS + 1, 2 * QSUB, KWIN), _F32),
            pltpu.VMEM((2, 2 * QSUB, KWIN), _F32),
            pltpu.VMEM((2, 2 * QSUB, LANES), _F32),
        ],
        compiler_params=pltpu.CompilerParams(
            dimension_semantics=("arbitrary",), vmem_limit_bytes=VMEM_LIMIT),
        name="mix_attn",
    )(h, g, w_in, w_kt, pool_w, pool_b, pool_scale, rb)


def _ff_chunks(d_ff):
    return tuple((off, min(FF_CHUNK, d_ff - off)) for off in range(0, d_ff, FF_CHUNK))


def _ffn_kernel(tiles_per_seq, d_ff, final, h_ref, a_ref, o_ref, p_ref, wout_ref, gf_ref, wup_ref, cw_ref,
                cb_ref, wdn_ref, gp_ref, wpg_ref, bpg_ref, wple_ref, fg_ref, out_ref,
                h1_s, hn_s, acc_s, gate_s, val_s, halo_s):
    @pl.when(pl.program_id(0) == 0)
    def _():
        halo_s[...] = jnp.zeros_like(halo_s)

    first = (pl.program_id(0) % tiles_per_seq) == 0
    chunks = _ff_chunks(d_ff)
    row_blocks = [slice(r, r + ROW_BLOCK) for r in range(0, TM, ROW_BLOCK)]

    def up(c, rows=slice(0, TM)):
        off, width = chunks[c]
        hn = hn_s[rows, :]
        gate_s[c % 2, CONV_HALO + rows.start:CONV_HALO + rows.stop, 0:width] = (
            _dot(hn, wup_ref[:, off:off + width]))
        val_s[c % 2, rows, 0:width] = _dot(hn, wup_ref[:, d_ff + off:d_ff + off + width])

    for rows in row_blocks:
        h1 = (h_ref[rows, :] + _dot(a_ref[rows, :], wout_ref[0:POOL_W, :])
              + _dot(o_ref[rows, :], wout_ref[POOL_W:POOL_W + ATT_W, :]))
        h1_s[rows, :] = h1
        hn_s[rows, :] = _rmsnorm(h1, gf_ref[...]).astype(_BF16)
    for rows in row_blocks:
        up(0, rows)

    def down(c):
        off, width = chunks[c]
        gate = gate_s.at[c % 2]
        gate[0:CONV_HALO, 0:width] = jnp.where(first, 0.0, halo_s[:, off:off + width])
        halo_s[:, off:off + width] = gate[TM:TM + CONV_HALO, 0:width]
        y = gate[CONV_HALO:CONV_HALO + TM, 0:width] * cw_ref[CONV_K - 1:CONV_K, off:off + width]
        y = y + cb_ref[:, off:off + width]
        for kk in range(CONV_K - 1):
            lag = CONV_K - 1 - kk
            y = y + gate[CONV_HALO - lag:CONV_HALO - lag + TM, 0:width] * cw_ref[kk:kk + 1, off:off + width]
        act = 0.5 * y * (1.0 + lax.erf(y * (1.0 / math.sqrt(2.0))))
        gated = (act * val_s[c % 2, :, 0:width]).astype(_BF16)
        for rows in row_blocks:
            part = _dot(gated[rows, :], wdn_ref[off:off + width, :])
            if c == 0:
                acc_s[rows, :] = part
            else:
                acc_s[rows, :] += part

    for c in range(len(chunks)):
        if c + 1 < len(chunks):
            for rows in row_blocks:
                up(c + 1, rows)
        down(c)

    for rows in row_blocks:
        h2 = h1_s[rows, :] + acc_s[rows, :]
        hn3 = _rmsnorm(h2, gp_ref[...]).astype(_BF16)
        gate = jax.nn.sigmoid(_dot(hn3, wpg_ref[...]) + bpg_ref[...])
        h3 = h2 + gate * _dot(p_ref[rows, :].astype(_BF16), wple_ref[...])
        out_ref[rows, :] = _rmsnorm(h3, fg_ref[...]) if final else h3


def _ffn(h, a, o, layer, p, w_out, g_ffn, w_up, conv_w, conv_b, w_down, g_ple, w_pg, b_pg, w_ple, final_g,
         tiles_per_seq, final):
    t, d = h.shape
    d_ff = w_down.shape[1]
    per_layer = (w_out, g_ffn, w_up, conv_w, conv_b, w_down, g_ple, w_pg, b_pg, w_ple)
    return pl.pallas_call(
        functools.partial(_ffn_kernel, tiles_per_seq, d_ff, final),
        grid=(t // TM,),
        in_specs=[_rows(d), _rows(POOL_W), _rows(ATT_W),
                  pl.BlockSpec((None, TM, p.shape[2]), lambda j: (layer, j, 0))]
        + [_resident(w.shape, layer) for w in per_layer] + [_resident(final_g.shape)],
        out_specs=_rows(d),
        out_shape=jax.ShapeDtypeStruct((t, d), _F32),
        scratch_shapes=[
            pltpu.VMEM((TM, d), _F32),
            pltpu.VMEM((TM, d), _BF16),
            pltpu.VMEM((TM, d), _F32),
            pltpu.VMEM((2, CONV_HALO + TM, FF_CHUNK), _F32),
            pltpu.VMEM((2, TM, FF_CHUNK), _F32),
            pltpu.VMEM((CONV_HALO, d_ff), _F32),
        ],
        compiler_params=pltpu.CompilerParams(
            dimension_semantics=("arbitrary",), vmem_limit_bytes=VMEM_LIMIT),
        name="ffn",
    )(h, a, o, p, *per_layer, final_g)


def kernel(x, p, norm_mix_g, w_in, pool_w, pool_b, pool_scale, rel_bias, w_out, norm_ffn_g, w_up,
           conv_w, conv_b, w_down, norm_ple_g, w_ple_gate, b_ple_gate, w_ple, final_g):
    b, s, d = x.shape
    depth = w_in.shape[0]
    d_ff = w_down.shape[1]
    assert s % TM == 0 and d_ff % LANES == 0 and w_up.shape[2] == 2 * d_ff
    assert w_in.shape[2] == POOL_W + 3 * ATT_W and w_out.shape[1] == POOL_W + ATT_W
    assert rel_bias.shape[1:] == (N_HEADS, 2 * MAX_REL + 1)
    t = b * s
    tiles_per_seq = s // TM

    w_in_b = w_in.astype(_BF16)
    w_k = lax.optimization_barrier(w_in[:, :, POOL_W + ATT_W:POOL_W + 2 * ATT_W])
    w_kt_b = w_k.transpose(0, 2, 1).astype(_BF16)
    pool_w_b = pool_w.astype(_BF16)
    w_out_b = w_out.astype(_BF16)
    w_up_b = w_up.astype(_BF16)
    w_down_b = w_down.astype(_BF16)
    w_pg_b = w_ple_gate.astype(_BF16)
    w_ple_b = w_ple.astype(_BF16)
    rb = rel_bias.astype(_F32)[:, :, :0:-1].reshape(depth, N_HEADS, 1, BAND)
    rows = lambda v: v.reshape(depth, 1, -1)
    p_rows = p.reshape(depth, t, -1)

    h = x.reshape(t, d)
    for i in range(depth):
        a, o = _mix_attn(h, i, rows(norm_mix_g), w_in_b, w_kt_b, pool_w_b, rows(pool_b), rows(pool_scale), rb,
                         tiles_per_seq)
        h = _ffn(h, a, o, i, p_rows, w_out_b, rows(norm_ffn_g), w_up_b, conv_w, rows(conv_b), w_down_b,
                 rows(norm_ple_g), w_pg_b, rows(b_ple_gate), w_ple_b, final_g.reshape(1, -1), tiles_per_seq,
                 final=(i == depth - 1))
    return h.reshape(b, s, d)
```

```python
import functools
import math

import jax
import jax.numpy as jnp
from jax import lax
from jax.experimental import pallas as pl
from jax.experimental.pallas import tpu as pltpu

CHUNK = 64
POOL_WINDOWS = (2, 4, 8, 16)
POOL_GROUP = 128
POOL_W = POOL_GROUP * len(POOL_WINDOWS)
N_HEADS = 8
HEAD_DIM = 64
ATT_W = N_HEADS * HEAD_DIM
LEFT_CHUNKS = 8
LEFT = LEFT_CHUNKS * CHUNK
MAX_REL = 128
CONV_K = 3
EPS = 1e-6

LANES = 128
SUBLANES = 8
TM = LEFT
QSUB = 2 * CHUNK
KWIN = QSUB + LEFT
N_PAIRS = N_HEADS * HEAD_DIM // LANES
N_SUBS = TM // QSUB
BAND = 2 * MAX_REL
ROW_BLOCK = 256
COL_BLOCK = 256
POOL_HALO = 16
CONV_HALO = SUBLANES
FF_CHUNK = 768
NEG = -1e30
VMEM_LIMIT = 56 * 1024 * 1024

_BF16 = jnp.bfloat16
_F32 = jnp.float32


def _dot(a, b):
    return jnp.dot(a, b, preferred_element_type=_F32)


def _dot_t(a, b):
    return lax.dot_general(a, b, (((1,), (1,)), ((), ())), preferred_element_type=_F32)


def _rmsnorm(x, g):
    ms = jnp.mean(x * x, axis=-1, keepdims=True)
    return x * lax.rsqrt(ms + EPS) * g


def _resident(shape, layer=None):
    if layer is None:
        return pl.BlockSpec(shape, lambda j: (0,) * len(shape), pipeline_mode=pl.Buffered(1))
    return pl.BlockSpec((None,) + tuple(shape[1:]), lambda j: (layer,) + (0,) * (len(shape) - 1),
                        pipeline_mode=pl.Buffered(1))


def _rows(width):
    return pl.BlockSpec((TM, width), lambda j: (j, 0))


def _interleave(xs, ys):
    out, taken = [], 0
    for i, x in enumerate(xs):
        out.append(x)
        want = (i + 1) * len(ys) // len(xs)
        out += ys[taken:want]
        taken = want
    return out


def _mix_in_pieces(tile, h_ref, g_ref, w_ref, wkt_ref, pw_ref, pb_ref, ps_ref, a_ref, ubuf, q_new, kt_new, v_new):
    row_blocks = [slice(r, r + ROW_BLOCK) for r in range(0, TM, ROW_BLOCK)]
    hn = {}

    def norm(rows):
        hn[rows.start] = _rmsnorm(h_ref[rows, :], g_ref[...]).astype(_BF16)

    def proj_u(rows, cols):
        ubuf[POOL_HALO + rows.start:POOL_HALO + rows.stop, cols] = _dot(hn[rows.start], w_ref[:, cols])

    def proj_q(rows, cols):
        q = _dot(hn[rows.start], w_ref[:, POOL_W + cols.start:POOL_W + cols.stop]) * (1.0 / math.sqrt(HEAD_DIM))
        q_new[rows, cols] = q.astype(_BF16)

    def proj_k(rows, cols):
        kt_new[cols, rows] = _dot_t(wkt_ref[cols, :], hn[rows.start]).astype(_BF16)

    def proj_v(rows, cols):
        base = POOL_W + 2 * ATT_W
        v = _dot(hn[rows.start], w_ref[:, base + cols.start:base + cols.stop]).astype(_BF16)
        for i in range(COL_BLOCK // LANES):
            pair = cols.start // LANES + i
            v_new[rows, 2 * pair * LANES:(2 * pair + 1) * LANES] = v[:, i * LANES:(i + 1) * LANES]
            v_new[rows, (2 * pair + 1) * LANES:(2 * pair + 2) * LANES] = jnp.ones((ROW_BLOCK, LANES), _BF16)

    def pool(gi):
        win = POOL_WINDOWS[gi]
        cols = slice(gi * POOL_GROUP, (gi + 1) * POOL_GROUP)
        u = ubuf[POOL_HALO:POOL_HALO + TM, cols]
        s = u
        for d in range(1, win):
            s = s + ubuf[POOL_HALO - d:POOL_HALO - d + TM, cols]
        pos = tile * TM + lax.broadcasted_iota(jnp.int32, (TM, 1), 0)
        cnt = jnp.minimum(pos + 1, win).astype(_F32)
        y = (s / cnt - u).astype(_BF16)
        r = _dot(y, pw_ref[gi]) + pb_ref[:, cols]
        a_ref[:, cols] = (r * ps_ref[:, cols]).astype(_BF16)

    part = functools.partial
    col_blocks = [slice(c, c + COL_BLOCK) for c in range(0, ATT_W, COL_BLOCK)]
    norms = [part(norm, rows) for rows in row_blocks]
    projections = [part(proj, rows, cols) for proj in (proj_u, proj_q, proj_k, proj_v)
                   for rows in row_blocks for cols in col_blocks]
    pools = [part(pool, gi) for gi in range(len(POOL_WINDOWS))]
    return norms, projections, pools


def _build_bias_tables(rb_ref, tbl):
    row = lax.broadcasted_iota(jnp.int32, (QSUB, KWIN), 0)
    col = lax.broadcasted_iota(jnp.int32, (QSUB, KWIN), 1)
    first_key = row & ~(CHUNK - 1)
    visible = (col >= first_key) & (col < first_key + (LEFT_CHUNKS + 1) * CHUNK)
    brow = lax.broadcasted_iota(jnp.int32, (QSUB, BAND), 0)
    bcol = lax.broadcasted_iota(jnp.int32, (QSUB, BAND), 1)
    for head in range(N_HEADS):
        r = rb_ref[head]
        far = r[:, 0:1]
        rolled = pltpu.roll(jnp.broadcast_to(r, (QSUB, BAND)), 0, 1, stride=1, stride_axis=0)
        band = jnp.where(bcol >= brow, rolled, far)
        full = jnp.concatenate([jnp.broadcast_to(far, (QSUB, KWIN - BAND)), band], axis=1)
        tbl[head // 2, (head % 2) * QSUB:(head % 2 + 1) * QSUB, :] = jnp.where(visible, full, NEG)


def _attention_pieces(with_prev, q_ref, ktp_ref, ktc_ref, vp_ref, vc_ref, tbl, s_s, m_s, o_ref):
    low_head = lax.broadcasted_iota(jnp.int32, (1, LANES), 1) < HEAD_DIM
    n_blocks = N_PAIRS * N_SUBS

    def geometry(n):
        pair, sub = divmod(n, N_SUBS)
        n_prev = TM - sub * QSUB
        return pair, slice(sub * QSUB, (sub + 1) * QSUB), slice(pair * LANES, (pair + 1) * LANES), n_prev

    def scores(n):
        pair, rows, cols, n_prev = geometry(n)
        q2 = q_ref[rows, cols]
        zero = jnp.zeros_like(q2)
        qq = jnp.concatenate([jnp.where(low_head, q2, zero), jnp.where(low_head, zero, q2)], axis=0)
        s_cur = _dot(qq, ktc_ref[cols, 0:KWIN - n_prev]) + tbl[pair, :, n_prev:KWIN]
        s_s[n % 2, :, n_prev:KWIN] = s_cur
        m = jnp.max(s_cur, axis=-1, keepdims=True)
        if with_prev:
            s_prev = _dot(qq, ktp_ref[cols, TM - n_prev:TM]) + tbl[pair, :, 0:n_prev]
            s_s[n % 2, :, 0:n_prev] = s_prev
            m = jnp.maximum(m, jnp.max(s_prev, axis=-1, keepdims=True))
        m_s[n % 2] = jnp.broadcast_to(m, (2 * QSUB, LANES))

    e = {}

    def weights(n):
        _, _, _, n_prev = geometry(n)
        m = m_s[n % 2]

        def exps(lo, hi):
            return jnp.concatenate([jnp.exp(s_s[n % 2, :, c:c + LANES] - m).astype(_BF16)
                                    for c in range(lo, hi, LANES)], axis=1)

        e[n] = (exps(0, n_prev) if with_prev else None, exps(n_prev, KWIN))

    def outputs(n):
        pair, rows, cols, n_prev = geometry(n)
        vcols = slice(2 * pair * LANES, (2 * pair + 2) * LANES)
        e_prev, e_cur = e.pop(n)
        ov = _dot(e_cur, vc_ref[0:KWIN - n_prev, vcols])
        if with_prev:
            ov = ov + _dot(e_prev, vp_ref[TM - n_prev:TM, vcols])
        o = ov[:, 0:LANES] / ov[:, LANES:2 * LANES]
        o_ref[rows, cols] = jnp.where(low_head, o[0:QSUB], o[QSUB:2 * QSUB]).astype(_BF16)

    part = functools.partial
    return [(part(scores, n), part(weights, n), part(outputs, n)) for n in range(n_blocks)]


def _mix_attn_kernel(tiles_per_seq, h_ref, g_ref, w_ref, wkt_ref, pw_ref, pb_ref, ps_ref, rb_ref,
                     a_ref, o_ref,
                     ubuf, q_cur, q_new, kt_prev, kt_cur, kt_new, v_prev, v_cur, v_new, tbl, s_s, m_s):
    step = pl.program_id(0)

    @pl.when(step == 0)
    def _():
        _build_bias_tables(rb_ref, tbl)
        ubuf[...] = jnp.zeros_like(ubuf)
        q_new[...] = jnp.zeros_like(q_new)
        kt_cur[...] = jnp.zeros_like(kt_cur)
        kt_new[...] = jnp.zeros_like(kt_new)
        v_cur[...] = jnp.ones_like(v_cur)
        v_new[...] = jnp.ones_like(v_new)

    kt_prev[...] = kt_cur[...]
    kt_cur[...] = kt_new[...]
    v_prev[...] = v_cur[...]
    v_cur[...] = v_new[...]
    q_cur[...] = q_new[...]

    tile = step % tiles_per_seq
    has_prev = ((step - 1) % tiles_per_seq) != 0
    ubuf[0:POOL_HALO, :] = jnp.where(tile == 0, 0.0, ubuf[TM:TM + POOL_HALO, :])

    def schedule(with_prev):
        norms, projections, pools = _mix_in_pieces(tile, h_ref, g_ref, w_ref, wkt_ref, pw_ref, pb_ref, ps_ref,
                                                   a_ref, ubuf, q_new, kt_new, v_new)
        attn = _attention_pieces(with_prev, q_cur, kt_prev, kt_cur, v_prev, v_cur, tbl, s_s, m_s, o_ref)
        for run in norms:
            run()
        n_u = len(projections) // 4
        queue = projections[:n_u] + _interleave(projections[n_u:], pools)
        attn[0][0]()
        for n, (_, weights, outputs) in enumerate(attn):
            weights()
            if n + 1 < len(attn):
                attn[n + 1][0]()
            for run in queue[n * len(queue) // len(attn):(n + 1) * len(queue) // len(attn)]:
                run()
            outputs()

    pl.when(has_prev)(functools.partial(schedule, True))
    pl.when(jnp.logical_not(has_prev))(functools.partial(schedule, False))


def _mix_attn(h, layer, g, w_in, w_kt, pool_w, pool_b, pool_scale, rb, tiles_per_seq):
    t, d = h.shape
    n_tiles = t // TM
    kt_buf = pltpu.VMEM((ATT_W, TM), _BF16)
    v_buf = pltpu.VMEM((TM, 2 * ATT_W), _BF16)
    q_buf = pltpu.VMEM((TM, ATT_W), _BF16)
    return pl.pallas_call(
        functools.partial(_mix_attn_kernel, tiles_per_seq),
        grid=(n_tiles + 1,),
        in_specs=[
            pl.BlockSpec((TM, d), lambda j: (jnp.minimum(j, n_tiles - 1), 0)),
            _resident(g.shape, layer),
            _resident(w_in.shape, layer),
            _resident(w_kt.shape, layer),
            _resident(pool_w.shape, layer),
            _resident(pool_b.shape, layer),
            _resident(pool_scale.shape, layer),
            _resident(rb.shape, layer),
        ],
        out_specs=[_rows(POOL_W), pl.BlockSpec((TM, ATT_W), lambda j: (jnp.maximum(j - 1, 0), 0))],
        out_shape=[jax.ShapeDtypeStruct((t + TM, POOL_W), _BF16), jax.ShapeDtypeStruct((t, ATT_W), _BF16)],
        scratch_shapes=[
            pltpu.VMEM((POOL_HALO + TM, POOL_W), _F32),
            q_buf, q_buf,
            kt_buf, kt_buf, kt_buf,
            v_buf, v_buf, v_buf,
            pltpu.VMEM((N_PAIRS, 2 * QSUB, KWIN), _F32),
            pltpu.VMEM((2, 2 * QSUB, KWIN), _F32),
            pltpu.VMEM((2, 2 * QSUB, LANES), _F32),
        ],
        compiler_params=pltpu.CompilerParams(
            dimension_semantics=("arbitrary",), vmem_limit_bytes=VMEM_LIMIT),
        name="mix_attn",
    )(h, g, w_in, w_kt, pool_w, pool_b, pool_scale, rb)


def _ff_chunks(d_ff):
    return tuple((off, min(FF_CHUNK, d_ff - off)) for off in range(0, d_ff, FF_CHUNK))


def _ffn_kernel(tiles_per_seq, d_ff, final, h_ref, a_ref, o_ref, p_ref, wout_ref, gf_ref, wup_ref, cw_ref,
                cb_ref, wdn_ref, gp_ref, wpg_ref, bpg_ref, wple_ref, fg_ref, out_ref,
                h1_s, hn_s, acc_s, gate_s, val_s, halo_s):
    @pl.when(pl.program_id(0) == 0)
    def _():
        halo_s[...] = jnp.zeros_like(halo_s)

    first = (pl.program_id(0) % tiles_per_seq) == 0
    chunks = _ff_chunks(d_ff)
    row_blocks = [slice(r, r + ROW_BLOCK) for r in range(0, TM, ROW_BLOCK)]

    def up(c, rows=slice(0, TM)):
        off, width = chunks[c]
        hn = hn_s[rows, :]
        gate_s[c % 2, CONV_HALO + rows.start:CONV_HALO + rows.stop, 0:width] = (
            _dot(hn, wup_ref[:, off:off + width]))
        val_s[c % 2, rows, 0:width] = _dot(hn, wup_ref[:, d_ff + off:d_ff + off + width])

    for rows in row_blocks:
        h1 = (h_ref[rows, :] + _dot(a_ref[rows, :], wout_ref[0:POOL_W, :])
              + _dot(o_ref[rows, :], wout_ref[POOL_W:POOL_W + ATT_W, :]))
        h1_s[rows, :] = h1
        hn_s[rows, :] = _rmsnorm(h1, gf_ref[...]).astype(_BF16)
    for rows in row_blocks:
        up(0, rows)

    def down(c):
        off, width = chunks[c]
        gate = gate_s.at[c % 2]
        gate[0:CONV_HALO, 0:width] = jnp.where(first, 0.0, halo_s[:, off:off + width])
        halo_s[:, off:off + width] = gate[TM:TM + CONV_HALO, 0:width]
        y = gate[CONV_HALO:CONV_HALO + TM, 0:width] * cw_ref[CONV_K - 1:CONV_K, off:off + width]
        y = y + cb_ref[:, off:off + width]
        for kk in range(CONV_K - 1):
            lag = CONV_K - 1 - kk
            y = y + gate[CONV_HALO - lag:CONV_HALO - lag + TM, 0:width] * cw_ref[kk:kk + 1, off:off + width]
        act = 0.5 * y * (1.0 + lax.erf(y * (1.0 / math.sqrt(2.0))))
        gated = (act * val_s[c % 2, :, 0:width]).astype(_BF16)
        for rows in row_blocks:
            part = _dot(gated[rows, :], wdn_ref[off:off + width, :])
            if c == 0:
                acc_s[rows, :] = part
            else:
                acc_s[rows, :] += part

    for c in range(len(chunks)):
        if c + 1 < len(chunks):
            for rows in row_blocks:
                up(c + 1, rows)
        down(c)

    for rows in row_blocks:
        h2 = h1_s[rows, :] + acc_s[rows, :]
        hn3 = _rmsnorm(h2, gp_ref[...]).astype(_BF16)
        gate = jax.nn.sigmoid(_dot(hn3, wpg_ref[...]) + bpg_ref[...])
        h3 = h2 + gate * _dot(p_ref[rows, :].astype(_BF16), wple_ref[...])
        out_ref[rows, :] = _rmsnorm(h3, fg_ref[...]) if final else h3


def _ffn(h, a, o, layer, p, w_out, g_ffn, w_up, conv_w, conv_b, w_down, g_ple, w_pg, b_pg, w_ple, final_g,
         tiles_per_seq, final):
    t, d = h.shape
    d_ff = w_down.shape[1]
    per_layer = (w_out, g_ffn, w_up, conv_w, conv_b, w_down, g_ple, w_pg, b_pg, w_ple)
    return pl.pallas_call(
        functools.partial(_ffn_kernel, tiles_per_seq, d_ff, final),
        grid=(t // TM,),
        in_specs=[_rows(d), _rows(POOL_W), _rows(ATT_W),
                  pl.BlockSpec((None, TM, p.shape[2]), lambda j: (layer, j, 0))]
        + [_resident(w.shape, layer) for w in per_layer] + [_resident(final_g.shape)],
        out_specs=_rows(d),
        out_shape=jax.ShapeDtypeStruct((t, d), _F32),
        scratch_shapes=[
            pltpu.VMEM((TM, d), _F32),
            pltpu.VMEM((TM, d), _BF16),
            pltpu.VMEM((TM, d), _F32),
            pltpu.VMEM((2, CONV_HALO + TM, FF_CHUNK), _F32),
            pltpu.VMEM((2, TM, FF_CHUNK), _F32),
            pltpu.VMEM((CONV_HALO, d_ff), _F32),
        ],
        compiler_params=pltpu.CompilerParams(
            dimension_semantics=("arbitrary",), vmem_limit_bytes=VMEM_LIMIT),
        name="ffn",
    )(h, a, o, p, *per_layer, final_g)


def kernel(x, p, norm_mix_g, w_in, pool_w, pool_b, pool_scale, rel_bias, w_out, norm_ffn_g, w_up,
           conv_w, conv_b, w_down, norm_ple_g, w_ple_gate, b_ple_gate, w_ple, final_g):
    b, s, d = x.shape
    depth = w_in.shape[0]
    d_ff = w_down.shape[1]
    assert s % TM == 0 and d_ff % LANES == 0 and w_up.shape[2] == 2 * d_ff
    assert w_in.shape[2] == POOL_W + 3 * ATT_W and w_out.shape[1] == POOL_W + ATT_W
    assert rel_bias.shape[1:] == (N_HEADS, 2 * MAX_REL + 1)
    t = b * s
    tiles_per_seq = s // TM

    w_in_b = w_in.astype(_BF16)
    w_k = lax.optimization_barrier(w_in[:, :, POOL_W + ATT_W:POOL_W + 2 * ATT_W])
    w_kt_b = w_k.transpose(0, 2, 1).astype(_BF16)
    pool_w_b = pool_w.astype(_BF16)
    w_out_b = w_out.astype(_BF16)
    w_up_b = w_up.astype(_BF16)
    w_down_b = w_down.astype(_BF16)
    w_pg_b = w_ple_gate.astype(_BF16)
    w_ple_b = w_ple.astype(_BF16)
    rb = rel_bias.astype(_F32)[:, :, :0:-1].reshape(depth, N_HEADS, 1, BAND)
    rows = lambda v: v.reshape(depth, 1, -1)
    p_rows = p.reshape(depth, t, -1)

    h = x.reshape(t, d)
    for i in range(depth):
        a, o = _mix_attn(h, i, rows(norm_mix_g), w_in_b, w_kt_b, pool_w_b, rows(pool_b), rows(pool_scale), rb,
                         tiles_per_seq)
        h = _ffn(h, a, o, i, p_rows, w_out_b, rows(norm_ffn_g), w_up_b, conv_w, rows(conv_b), w_down_b,
                 rows(norm_ple_g), w_pg_b, rows(b_ple_gate), w_ple_b, final_g.reshape(1, -1), tiles_per_seq,
                 final=(i == depth - 1))
    return h.reshape(b, s, d)
```

```python
import functools
import math

import jax
import jax.numpy as jnp
from jax import lax
from jax.experimental import pallas as pl
from jax.experimental.pallas import tpu as pltpu

CHUNK = 64
POOL_WINDOWS = (2, 4, 8, 16)
POOL_GROUP = 128
POOL_W = POOL_GROUP * len(POOL_WINDOWS)
N_HEADS = 8
HEAD_DIM = 64
ATT_W = N_HEADS * HEAD_DIM
LEFT_CHUNKS = 8
LEFT = LEFT_CHUNKS * CHUNK
MAX_REL = 128
CONV_K = 3
EPS = 1e-6

LANES = 128
SUBLANES = 8
TM = LEFT
QSUB = 2 * CHUNK
KWIN = QSUB + LEFT
N_PAIRS = N_HEADS * HEAD_DIM // LANES
N_SUBS = TM // QSUB
BAND = 2 * MAX_REL
ROW_BLOCK = 256
COL_BLOCK = 256
POOL_HALO = 16
CONV_HALO = SUBLANES
FF_CHUNK = 768
NEG = -1e30
VMEM_LIMIT = 56 * 1024 * 1024

_BF16 = jnp.bfloat16
_F32 = jnp.float32


def _dot(a, b):
    return jnp.dot(a, b, preferred_element_type=_F32)


def _dot_t(a, b):
    return lax.dot_general(a, b, (((1,), (1,)), ((), ())), preferred_element_type=_F32)


def _rmsnorm(x, g):
    ms = jnp.mean(x * x, axis=-1, keepdims=True)
    return x * lax.rsqrt(ms + EPS) * g


def _resident(shape, layer=None):
    if layer is None:
        return pl.BlockSpec(shape, lambda j: (0,) * len(shape), pipeline_mode=pl.Buffered(1))
    return pl.BlockSpec((None,) + tuple(shape[1:]), lambda j: (layer,) + (0,) * (len(shape) - 1),
                        pipeline_mode=pl.Buffered(1))


def _rows(width):
    return pl.BlockSpec((TM, width), lambda j: (j, 0))


def _interleave(xs, ys):
    out, taken = [], 0
    for i, x in enumerate(xs):
        out.append(x)
        want = (i + 1) * len(ys) // len(xs)
        out += ys[taken:want]
        taken = want
    return out


def _mix_in_pieces(tile, h_ref, g_ref, w_ref, wkt_ref, pw_ref, pb_ref, ps_ref, a_ref, ubuf, q_new, kt_new, v_new):
    row_blocks = [slice(r, r + ROW_BLOCK) for r in range(0, TM, ROW_BLOCK)]
    hn = {}

    def norm(rows):
        hn[rows.start] = _rmsnorm(h_ref[rows, :], g_ref[...]).astype(_BF16)

    def proj_u(rows, cols):
        ubuf[POOL_HALO + rows.start:POOL_HALO + rows.stop, cols] = _dot(hn[rows.start], w_ref[:, cols])

    def proj_q(rows, cols):
        q = _dot(hn[rows.start], w_ref[:, POOL_W + cols.start:POOL_W + cols.stop]) * (1.0 / math.sqrt(HEAD_DIM))
        q_new[rows, cols] = q.astype(_BF16)

    def proj_k(rows, cols):
        kt_new[cols, rows] = _dot_t(wkt_ref[cols, :], hn[rows.start]).astype(_BF16)

    def proj_v(rows, cols):
        base = POOL_W + 2 * ATT_W
        v = _dot(hn[rows.start], w_ref[:, base + cols.start:base + cols.stop]).astype(_BF16)
        for i in range(COL_BLOCK // LANES):
            pair = cols.start // LANES + i
            v_new[rows, 2 * pair * LANES:(2 * pair + 1) * LANES] = v[:, i * LANES:(i + 1) * LANES]
            v_new[rows, (2 * pair + 1) * LANES:(2 * pair + 2) * LANES] = jnp.ones((ROW_BLOCK, LANES), _BF16)

    def pool(gi):
        win = POOL_WINDOWS[gi]
        cols = slice(gi * POOL_GROUP, (gi + 1) * POOL_GROUP)
        u = ubuf[POOL_HALO:POOL_HALO + TM, cols]
        s = u
        for d in range(1, win):
            s = s + ubuf[POOL_HALO - d:POOL_HALO - d + TM, cols]
        pos = tile * TM + lax.broadcasted_iota(jnp.int32, (TM, 1), 0)
        cnt = jnp.minimum(pos + 1, win).astype(_F32)
        y = (s / cnt - u).astype(_BF16)
        r = _dot(y, pw_ref[gi]) + pb_ref[:, cols]
        a_ref[:, cols] = (r * ps_ref[:, cols]).astype(_BF16)

    part = functools.partial
    col_blocks = [slice(c, c + COL_BLOCK) for c in range(0, ATT_W, COL_BLOCK)]
    norms = [part(norm, rows) for rows in row_blocks]
    projections = [part(proj, rows, cols) for proj in (proj_u, proj_q, proj_k, proj_v)
                   for rows in row_blocks for cols in col_blocks]
    pools = [part(pool, gi) for gi in range(len(POOL_WINDOWS))]
    return norms, projections, pools


def _build_bias_tables(rb_ref, tbl):
    row = lax.broadcasted_iota(jnp.int32, (QSUB, KWIN), 0)
    col = lax.broadcasted_iota(jnp.int32, (QSUB, KWIN), 1)
    first_key = row & ~(CHUNK - 1)
    visible = (col >= first_key) & (col < first_key + (LEFT_CHUNKS + 1) * CHUNK)
    brow = lax.broadcasted_iota(jnp.int32, (QSUB, BAND), 0)
    bcol = lax.broadcasted_iota(jnp.int32, (QSUB, BAND), 1)
    for head in range(N_HEADS):
        r = rb_ref[head]
        far = r[:, 0:1]
        rolled = pltpu.roll(jnp.broadcast_to(r, (QSUB, BAND)), 0, 1, stride=1, stride_axis=0)
        band = jnp.where(bcol >= brow, rolled, far)
        full = jnp.concatenate([jnp.broadcast_to(far, (QSUB, KWIN - BAND)), band], axis=1)
        tbl[head // 2, (head % 2) * QSUB:(head % 2 + 1) * QSUB, :] = jnp.where(visible, full, NEG)


def _attention_pieces(with_prev, q_ref, ktp_ref, ktc_ref, vp_ref, vc_ref, tbl, s_s, m_s, o_ref):
    low_head = lax.broadcasted_iota(jnp.int32, (1, LANES), 1) < HEAD_DIM
    n_blocks = N_PAIRS * N_SUBS

    def geometry(n):
        pair, sub = divmod(n, N_SUBS)
        n_prev = TM - sub * QSUB
        return pair, slice(sub * QSUB, (sub + 1) * QSUB), slice(pair * LANES, (pair + 1) * LANES), n_prev

    def scores(n):
        pair, rows, cols, n_prev = geometry(n)
        q2 = q_ref[rows, cols]
        zero = jnp.zeros_like(q2)
        qq = jnp.concatenate([jnp.where(low_head, q2, zero), jnp.where(low_head, zero, q2)], axis=0)
        s_cur = _dot(qq, ktc_ref[cols, 0:KWIN - n_prev]) + tbl[pair, :, n_prev:KWIN]
        s_s[n % 2, :, n_prev:KWIN] = s_cur
        m = jnp.max(s_cur, axis=-1, keepdims=True)
        if with_prev:
            s_prev = _dot(qq, ktp_ref[cols, TM - n_prev:TM]) + tbl[pair, :, 0:n_prev]
            s_s[n % 2, :, 0:n_prev] = s_prev
            m = jnp.maximum(m, jnp.max(s_prev, axis=-1, keepdims=True))
        m_s[n % 2] = jnp.broadcast_to(m, (2 * QSUB, LANES))

    e = {}

    def weights(n):
        _, _, _, n_prev = geometry(n)
        m = m_s[n % 2]

        def exps(lo, hi):
            return jnp.concatenate([jnp.exp(s_s[n % 2, :, c:c + LANES] - m).astype(_BF16)
                                    for c in range(lo, hi, LANES)], axis=1)

        e[n] = (exps(0, n_prev) if with_prev else None, exps(n_prev, KWIN))

    def outputs(n):
        pair, rows, cols, n_prev = geometry(n)
        vcols = slice(2 * pair * LANES, (2 * pair + 2) * LANES)
        e_prev, e_cur = e.pop(n)
        ov = _dot(e_cur, vc_ref[0:KWIN - n_prev, vcols])
        if with_prev:
            ov = ov + _dot(e_prev, vp_ref[TM - n_prev:TM, vcols])
        o = ov[:, 0:LANES] / ov[:, LANES:2 * LANES]
        o_ref[rows, cols] = jnp.where(low_head, o[0:QSUB], o[QSUB:2 * QSUB]).astype(_BF16)

    part = functools.partial
    return [(part(scores, n), part(weights, n), part(outputs, n)) for n in range(n_blocks)]


def _mix_attn_kernel(tiles_per_seq, h_ref, g_ref, w_ref, wkt_ref, pw_ref, pb_ref, ps_ref, rb_ref,
                     a_ref, o_ref,
                     ubuf, q_cur, q_new, kt_prev, kt_cur, kt_new, v_prev, v_cur, v_new, tbl, s_s, m_s):
    step = pl.program_id(0)

    @pl.when(step == 0)
    def _():
        _build_bias_tables(rb_ref, tbl)
        ubuf[...] = jnp.zeros_like(ubuf)
        q_new[...] = jnp.zeros_like(q_new)
        kt_cur[...] = jnp.zeros_like(kt_cur)
        kt_new[...] = jnp.zeros_like(kt_new)
        v_cur[...] = jnp.ones_like(v_cur)
        v_new[...] = jnp.ones_like(v_new)

    tile = step % tiles_per_seq
    has_prev = ((step - 1) % tiles_per_seq) != 0

    def schedule(with_prev):
        if with_prev:
            kt_prev[...] = kt_cur[...]
            v_prev[...] = v_cur[...]
        kt_cur[...] = kt_new[...]
        v_cur[...] = v_new[...]
        q_cur[...] = q_new[...]
        ubuf[0:POOL_HALO, :] = jnp.where(tile == 0, 0.0, ubuf[TM:TM + POOL_HALO, :])

        norms, projections, pools = _mix_in_pieces(tile, h_ref, g_ref, w_ref, wkt_ref, pw_ref, pb_ref, ps_ref,
                                                   a_ref, ubuf, q_new, kt_new, v_new)
        attn = _attention_pieces(with_prev, q_cur, kt_prev, kt_cur, v_prev, v_cur, tbl, s_s, m_s, o_ref)
        for run in norms:
            run()
        n_u = len(projections) // 4
        queue = projections[:n_u] + _interleave(projections[n_u:], pools)
        attn[0][0]()
        for n, (_, weights, outputs) in enumerate(attn):
            weights()
            if n + 1 < len(attn):
                attn[n + 1][0]()
            for run in queue[n * len(queue) // len(attn):(n + 1) * len(queue) // len(attn)]:
                run()
            outputs()

    pl.when(has_prev)(functools.partial(schedule, True))
    pl.when(jnp.logical_not(has_prev))(functools.partial(schedule, False))


def _mix_attn(h, layer, g, w_in, w_kt, pool_w, pool_b, pool_scale, rb, tiles_per_seq):
    t, d = h.shape
    n_tiles = t // TM
    kt_buf = pltpu.VMEM((ATT_W, TM), _BF16)
    v_buf = pltpu.VMEM((TM, 2 * ATT_W), _BF16)
    q_buf = pltpu.VMEM((TM, ATT_W), _BF16)
    return pl.pallas_call(
        functools.partial(_mix_attn_kernel, tiles_per_seq),
        grid=(n_tiles + 1,),
        in_specs=[
            pl.BlockSpec((TM, d), lambda j: (jnp.minimum(j, n_tiles - 1), 0)),
            _resident(g.shape, layer),
            _resident(w_in.shape, layer),
            _resident(w_kt.shape, layer),
            _resident(pool_w.shape, layer),
            _resident(pool_b.shape, layer),
            _resident(pool_scale.shape, layer),
            _resident(rb.shape, layer),
        ],
        out_specs=[_rows(POOL_W), pl.BlockSpec((TM, ATT_W), lambda j: (jnp.maximum(j - 1, 0), 0))],
        out_shape=[jax.ShapeDtypeStruct((t + TM, POOL_W), _BF16), jax.ShapeDtypeStruct((t, ATT_W), _BF16)],
        scratch_shapes=[
            pltpu.VMEM((POOL_HALO + TM, POOL_W), _F32),
            q_buf, q_buf,
            kt_buf, kt_buf, kt_buf,
            v_buf, v_buf, v_buf,
            pltpu.VMEM((N_PAIRS, 2 * QSUB, KWIN), _F32),
            pltpu.VMEM((2, 2 * QSUB, KWIN), _F32),
            pltpu.VMEM((2, 2 * QSUB, LANES), _F32),
        ],
        compiler_params=pltpu.CompilerParams(
            dimension_semantics=("arbitrary",), vmem_limit_bytes=VMEM_LIMIT),
        name="mix_attn",
    )(h, g, w_in, w_kt, pool_w, pool_b, pool_scale, rb)


def _ff_chunks(d_ff):
    return tuple((off, min(FF_CHUNK, d_ff - off)) for off in range(0, d_ff, FF_CHUNK))


def _ffn_kernel(tiles_per_seq, d_ff, final, h_ref, a_ref, o_ref, p_ref, wout_ref, gf_ref, wup_ref, cw_ref,
                cb_ref, wdn_ref, gp_ref, wpg_ref, bpg_ref, wple_ref, fg_ref, out_ref,
                h1_s, hn_s, acc_s, gate_s, val_s, halo_s):
    @pl.when(pl.program_id(0) == 0)
    def _():
        halo_s[...] = jnp.zeros_like(halo_s)

    first = (pl.program_id(0) % tiles_per_seq) == 0
    chunks = _ff_chunks(d_ff)
    row_blocks = [slice(r, r + ROW_BLOCK) for r in range(0, TM, ROW_BLOCK)]

    def up(c, rows=slice(0, TM)):
        off, width = chunks[c]
        hn = hn_s[rows, :]
        gate_s[c % 2, CONV_HALO + rows.start:CONV_HALO + rows.stop, 0:width] = (
            _dot(hn, wup_ref[:, off:off + width]))
        val_s[c % 2, rows, 0:width] = _dot(hn, wup_ref[:, d_ff + off:d_ff + off + width])

    for rows in row_blocks:
        h1 = (h_ref[rows, :] + _dot(a_ref[rows, :], wout_ref[0:POOL_W, :])
              + _dot(o_ref[rows, :], wout_ref[POOL_W:POOL_W + ATT_W, :]))
        h1_s[rows, :] = h1
        hn_s[rows, :] = _rmsnorm(h1, gf_ref[...]).astype(_BF16)
    for rows in row_blocks:
        up(0, rows)

    def down(c):
        off, width = chunks[c]
        gate = gate_s.at[c % 2]
        gate[0:CONV_HALO, 0:width] = jnp.where(first, 0.0, halo_s[:, off:off + width])
        halo_s[:, off:off + width] = gate[TM:TM + CONV_HALO, 0:width]
        y = gate[CONV_HALO:CONV_HALO + TM, 0:width] * cw_ref[CONV_K - 1:CONV_K, off:off + width]
        y = y + cb_ref[:, off:off + width]
        for kk in range(CONV_K - 1):
            lag = CONV_K - 1 - kk
            y = y + gate[CONV_HALO - lag:CONV_HALO - lag + TM, 0:width] * cw_ref[kk:kk + 1, off:off + width]
        act = 0.5 * y * (1.0 + lax.erf(y * (1.0 / math.sqrt(2.0))))
        gated = (act * val_s[c % 2, :, 0:width]).astype(_BF16)
        for rows in row_blocks:
            part = _dot(gated[rows, :], wdn_ref[off:off + width, :])
            if c == 0:
                acc_s[rows, :] = part
            else:
                acc_s[rows, :] += part

    for c in range(len(chunks)):
        if c + 1 < len(chunks):
            for rows in row_blocks:
                up(c + 1, rows)
        down(c)

    for rows in row_blocks:
        h2 = h1_s[rows, :] + acc_s[rows, :]
        hn3 = _rmsnorm(h2, gp_ref[...]).astype(_BF16)
        gate = jax.nn.sigmoid(_dot(hn3, wpg_ref[...]) + bpg_ref[...])
        h3 = h2 + gate * _dot(p_ref[rows, :].astype(_BF16), wple_ref[...])
        out_ref[rows, :] = _rmsnorm(h3, fg_ref[...]) if final else h3


def _ffn(h, a, o, layer, p, w_out, g_ffn, w_up, conv_w, conv_b, w_down, g_ple, w_pg, b_pg, w_ple, final_g,
         tiles_per_seq, final):
    t, d = h.shape
    d_ff = w_down.shape[1]
    per_layer = (w_out, g_ffn, w_up, conv_w, conv_b, w_down, g_ple, w_pg, b_pg, w_ple)
    return pl.pallas_call(
        functools.partial(_ffn_kernel, tiles_per_seq, d_ff, final),
        grid=(t // TM,),
        in_specs=[_rows(d), _rows(POOL_W), _rows(ATT_W),
                  pl.BlockSpec((None, TM, p.shape[2]), lambda j: (layer, j, 0))]
        + [_resident(w.shape, layer) for w in per_layer] + [_resident(final_g.shape)],
        out_specs=_rows(d),
        out_shape=jax.ShapeDtypeStruct((t, d), _F32),
        scratch_shapes=[
            pltpu.VMEM((TM, d), _F32),
            pltpu.VMEM((TM, d), _BF16),
            pltpu.VMEM((TM, d), _F32),
            pltpu.VMEM((2, CONV_HALO + TM, FF_CHUNK), _F32),
            pltpu.VMEM((2, TM, FF_CHUNK), _F32),
            pltpu.VMEM((CONV_HALO, d_ff), _F32),
        ],
        compiler_params=pltpu.CompilerParams(
            dimension_semantics=("arbitrary",), vmem_limit_bytes=VMEM_LIMIT),
        name="ffn",
    )(h, a, o, p, *per_layer, final_g)


def kernel(x, p, norm_mix_g, w_in, pool_w, pool_b, pool_scale, rel_bias, w_out, norm_ffn_g, w_up,
           conv_w, conv_b, w_down, norm_ple_g, w_ple_gate, b_ple_gate, w_ple, final_g):
    b, s, d = x.shape
    depth = w_in.shape[0]
    d_ff = w_down.shape[1]
    assert s % TM == 0 and d_ff % LANES == 0 and w_up.shape[2] == 2 * d_ff
    assert w_in.shape[2] == POOL_W + 3 * ATT_W and w_out.shape[1] == POOL_W + ATT_W
    assert rel_bias.shape[1:] == (N_HEADS, 2 * MAX_REL + 1)
    t = b * s
    tiles_per_seq = s // TM

    w_in_b = w_in.astype(_BF16)
    w_k = lax.optimization_barrier(w_in[:, :, POOL_W + ATT_W:POOL_W + 2 * ATT_W])
    w_kt_b = w_k.transpose(0, 2, 1).astype(_BF16)
    pool_w_b = pool_w.astype(_BF16)
    w_out_b = w_out.astype(_BF16)
    w_up_b = w_up.astype(_BF16)
    w_down_b = w_down.astype(_BF16)
    w_pg_b = w_ple_gate.astype(_BF16)
    w_ple_b = w_ple.astype(_BF16)
    rb = rel_bias.astype(_F32)[:, :, :0:-1].reshape(depth, N_HEADS, 1, BAND)
    rows = lambda v: v.reshape(depth, 1, -1)
    p_rows = p.reshape(depth, t, -1)

    h = x.reshape(t, d)
    for i in range(depth):
        a, o = _mix_attn(h, i, rows(norm_mix_g), w_in_b, w_kt_b, pool_w_b, rows(pool_b), rows(pool_scale), rb,
                         tiles_per_seq)
        h = _ffn(h, a, o, i, p_rows, w_out_b, rows(norm_ffn_g), w_up_b, conv_w, rows(conv_b), w_down_b,
                 rows(norm_ple_g), w_pg_b, rows(b_ple_gate), w_ple_b, final_g.reshape(1, -1), tiles_per_seq,
                 final=(i == depth - 1))
    return h.reshape(b, s, d)
```

```python
import functools
import math

import jax
import jax.numpy as jnp
from jax import lax
from jax.experimental import pallas as pl
from jax.experimental.pallas import tpu as pltpu

CHUNK = 64
POOL_WINDOWS = (2, 4, 8, 16)
POOL_GROUP = 128
POOL_W = POOL_GROUP * len(POOL_WINDOWS)
N_HEADS = 8
HEAD_DIM = 64
ATT_W = N_HEADS * HEAD_DIM
LEFT_CHUNKS = 8
LEFT = LEFT_CHUNKS * CHUNK
MAX_REL = 128
CONV_K = 3
EPS = 1e-6

LANES = 128
SUBLANES = 8
TM = LEFT
QSUB = 2 * CHUNK
KWIN = QSUB + LEFT
N_PAIRS = N_HEADS * HEAD_DIM // LANES
N_SUBS = TM // QSUB
BAND = 2 * MAX_REL
ROW_BLOCK = 256
COL_BLOCK = 256
POOL_HALO = 16
CONV_HALO = SUBLANES
FF_CHUNK = 768
NEG = -1e30
VMEM_LIMIT = 56 * 1024 * 1024

_BF16 = jnp.bfloat16
_F32 = jnp.float32


def _dot(a, b):
    return jnp.dot(a, b, preferred_element_type=_F32)


def _dot_t(a, b):
    return lax.dot_general(a, b, (((1,), (1,)), ((), ())), preferred_element_type=_F32)


def _rmsnorm(x, g):
    ms = jnp.mean(x * x, axis=-1, keepdims=True)
    return x * lax.rsqrt(ms + EPS) * g


def _resident(shape, layer=None):
    if layer is None:
        return pl.BlockSpec(shape, lambda j: (0,) * len(shape), pipeline_mode=pl.Buffered(1))
    return pl.BlockSpec((None,) + tuple(shape[1:]), lambda j: (layer,) + (0,) * (len(shape) - 1),
                        pipeline_mode=pl.Buffered(1))


def _rows(width):
    return pl.BlockSpec((TM, width), lambda j: (j, 0))


def _interleave(xs, ys):
    out, taken = [], 0
    for i, x in enumerate(xs):
        out.append(x)
        want = (i + 1) * len(ys) // len(xs)
        out += ys[taken:want]
        taken = want
    return out


def _mix_in_pieces(tile, h_ref, g_ref, w_ref, wkt_ref, pw_ref, pb_ref, ps_ref, a_ref, ubuf, q_new, kt_new, v_new):
    row_blocks = [slice(r, r + ROW_BLOCK) for r in range(0, TM, ROW_BLOCK)]
    hn = {}

    def norm(rows):
        hn[rows.start] = _rmsnorm(h_ref[rows, :], g_ref[...]).astype(_BF16)

    def proj_u(rows, cols):
        ubuf[POOL_HALO + rows.start:POOL_HALO + rows.stop, cols] = _dot(hn[rows.start], w_ref[:, cols])

    def proj_q(rows, cols):
        q = _dot(hn[rows.start], w_ref[:, POOL_W + cols.start:POOL_W + cols.stop]) * (1.0 / math.sqrt(HEAD_DIM))
        q_new[rows, cols] = q.astype(_BF16)

    def proj_k(rows, cols):
        kt_new[cols, rows] = _dot_t(wkt_ref[cols, :], hn[rows.start]).astype(_BF16)

    def proj_v(rows, cols):
        base = POOL_W + 2 * ATT_W
        v = _dot(hn[rows.start], w_ref[:, base + cols.start:base + cols.stop]).astype(_BF16)
        for i in range(COL_BLOCK // LANES):
            pair = cols.start // LANES + i
            v_new[rows, 2 * pair * LANES:(2 * pair + 1) * LANES] = v[:, i * LANES:(i + 1) * LANES]
            v_new[rows, (2 * pair + 1) * LANES:(2 * pair + 2) * LANES] = jnp.ones((ROW_BLOCK, LANES), _BF16)

    def pool(gi):
        win = POOL_WINDOWS[gi]
        cols = slice(gi * POOL_GROUP, (gi + 1) * POOL_GROUP)
        u = ubuf[POOL_HALO:POOL_HALO + TM, cols]
        s = u
        for d in range(1, win):
            s = s + ubuf[POOL_HALO - d:POOL_HALO - d + TM, cols]
        pos = tile * TM + lax.broadcasted_iota(jnp.int32, (TM, 1), 0)
        cnt = jnp.minimum(pos + 1, win).astype(_F32)
        y = (s / cnt - u).astype(_BF16)
        r = _dot(y, pw_ref[gi]) + pb_ref[:, cols]
        a_ref[:, cols] = (r * ps_ref[:, cols]).astype(_BF16)

    part = functools.partial
    col_blocks = [slice(c, c + COL_BLOCK) for c in range(0, ATT_W, COL_BLOCK)]
    norms = [part(norm, rows) for rows in row_blocks]
    projections = [part(proj, rows, cols) for proj in (proj_u, proj_q, proj_k, proj_v)
                   for rows in row_blocks for cols in col_blocks]
    pools = [part(pool, gi) for gi in range(len(POOL_WINDOWS))]
    return norms, projections, pools


def _build_bias_tables(rb_ref, tbl):
    row = lax.broadcasted_iota(jnp.int32, (QSUB, KWIN), 0)
    col = lax.broadcasted_iota(jnp.int32, (QSUB, KWIN), 1)
    first_key = row & ~(CHUNK - 1)
    visible = (col >= first_key) & (col < first_key + (LEFT_CHUNKS + 1) * CHUNK)
    brow = lax.broadcasted_iota(jnp.int32, (QSUB, BAND), 0)
    bcol = lax.broadcasted_iota(jnp.int32, (QSUB, BAND), 1)
    for head in range(N_HEADS):
        r = rb_ref[head]
        far = r[:, 0:1]
        rolled = pltpu.roll(jnp.broadcast_to(r, (QSUB, BAND)), 0, 1, stride=1, stride_axis=0)
        band = jnp.where(bcol >= brow, rolled, far)
        full = jnp.concatenate([jnp.broadcast_to(far, (QSUB, KWIN - BAND)), band], axis=1)
        tbl[head // 2, (head % 2) * QSUB:(head % 2 + 1) * QSUB, :] = jnp.where(visible, full, NEG)


def _attention_pieces(with_prev, q_ref, ktp_ref, ktc_ref, vp_ref, vc_ref, tbl, s_s, m_s, o_ref):
    low_head = lax.broadcasted_iota(jnp.int32, (1, LANES), 1) < HEAD_DIM
    n_blocks = N_PAIRS * N_SUBS

    def geometry(n):
        pair, sub = divmod(n, N_SUBS)
        n_prev = TM - sub * QSUB
        return pair, slice(sub * QSUB, (sub + 1) * QSUB), slice(pair * LANES, (pair + 1) * LANES), n_prev

    def scores(n):
        pair, rows, cols, n_prev = geometry(n)
        q2 = q_ref[rows, cols]
        zero = jnp.zeros_like(q2)
        qq = jnp.concatenate([jnp.where(low_head, q2, zero), jnp.where(low_head, zero, q2)], axis=0)
        s_cur = _dot(qq, ktc_ref[cols, 0:KWIN - n_prev]) + tbl[pair, :, n_prev:KWIN]
        s_s[n % 2, :, n_prev:KWIN] = s_cur
        m = jnp.max(s_cur, axis=-1, keepdims=True)
        if with_prev:
            s_prev = _dot(qq, ktp_ref[cols, TM - n_prev:TM]) + tbl[pair, :, 0:n_prev]
            s_s[n % 2, :, 0:n_prev] = s_prev
            m = jnp.maximum(m, jnp.max(s_prev, axis=-1, keepdims=True))
        m_s[n % 2] = jnp.broadcast_to(m, (2 * QSUB, LANES))

    e = {}

    def weights(n):
        _, _, _, n_prev = geometry(n)
        m = m_s[n % 2]

        def exps(lo, hi):
            return jnp.concatenate([jnp.exp(s_s[n % 2, :, c:c + LANES] - m).astype(_BF16)
                                    for c in range(lo, hi, LANES)], axis=1)

        e[n] = (exps(0, n_prev) if with_prev else None, exps(n_prev, KWIN))

    def outputs(n):
        pair, rows, cols, n_prev = geometry(n)
        vcols = slice(2 * pair * LANES, (2 * pair + 2) * LANES)
        e_prev, e_cur = e.pop(n)
        ov = _dot(e_cur, vc_ref[0:KWIN - n_prev, vcols])
        if with_prev:
            ov = ov + _dot(e_prev, vp_ref[TM - n_prev:TM, vcols])
        o = ov[:, 0:LANES] / ov[:, LANES:2 * LANES]
        o_ref[rows, cols] = jnp.where(low_head, o[0:QSUB], o[QSUB:2 * QSUB]).astype(_BF16)

    part = functools.partial
    return [(part(scores, n), part(weights, n), part(outputs, n)) for n in range(n_blocks)]


def _mix_attn_kernel(tiles_per_seq, n_tiles, h_ref, g_ref, w_ref, wkt_ref, pw_ref, pb_ref, ps_ref, rb_ref,
                     a_ref, o_ref,
                     ubuf, q_cur, q_new, kt_prev, kt_cur, kt_new, v_prev, v_cur, v_new, tbl, s_s, m_s):
    step = pl.program_id(0)

    @pl.when(step == 0)
    def _():
        _build_bias_tables(rb_ref, tbl)
        ubuf[...] = jnp.zeros_like(ubuf)

    tile = step % tiles_per_seq

    def schedule(mix, attend, with_prev):
        norms, projections, pools, blocks = [], [], [], []
        if attend:
            if with_prev:
                kt_prev[...] = kt_cur[...]
                v_prev[...] = v_cur[...]
            kt_cur[...] = kt_new[...]
            v_cur[...] = v_new[...]
            q_cur[...] = q_new[...]
            blocks = _attention_pieces(with_prev, q_cur, kt_prev, kt_cur, v_prev, v_cur, tbl, s_s, m_s, o_ref)
        if mix:
            ubuf[0:POOL_HALO, :] = jnp.where(tile == 0, 0.0, ubuf[TM:TM + POOL_HALO, :])
            norms, projections, pools = _mix_in_pieces(tile, h_ref, g_ref, w_ref, wkt_ref, pw_ref, pb_ref,
                                                       ps_ref, a_ref, ubuf, q_new, kt_new, v_new)
        for run in norms:
            run()
        n_u = len(projections) // 4
        queue = projections[:n_u] + _interleave(projections[n_u:], pools)
        if not blocks:
            for run in queue:
                run()
            return
        blocks[0][0]()
        for n, (_, weights, outputs) in enumerate(blocks):
            weights()
            if n + 1 < len(blocks):
                blocks[n + 1][0]()
            for run in queue[n * len(queue) // len(blocks):(n + 1) * len(queue) // len(blocks)]:
                run()
            outputs()

    has_prev = ((step - 1) % tiles_per_seq) != 0
    middle = jnp.logical_and(step > 0, step < n_tiles)
    pl.when(step == 0)(functools.partial(schedule, True, False, False))
    pl.when(jnp.logical_and(middle, has_prev))(functools.partial(schedule, True, True, True))
    pl.when(jnp.logical_and(middle, jnp.logical_not(has_prev)))(functools.partial(schedule, True, True, False))
    pl.when(step == n_tiles)(functools.partial(schedule, False, True, (n_tiles - 1) % tiles_per_seq != 0))


def _mix_attn(h, layer, g, w_in, w_kt, pool_w, pool_b, pool_scale, rb, tiles_per_seq):
    t, d = h.shape
    n_tiles = t // TM
    kt_buf = pltpu.VMEM((ATT_W, TM), _BF16)
    v_buf = pltpu.VMEM((TM, 2 * ATT_W), _BF16)
    q_buf = pltpu.VMEM((TM, ATT_W), _BF16)
    return pl.pallas_call(
        functools.partial(_mix_attn_kernel, tiles_per_seq, n_tiles),
        grid=(n_tiles + 1,),
        in_specs=[
            pl.BlockSpec((TM, d), lambda j: (jnp.minimum(j, n_tiles - 1), 0)),
            _resident(g.shape, layer),
            _resident(w_in.shape, layer),
            _resident(w_kt.shape, layer),
            _resident(pool_w.shape, layer),
            _resident(pool_b.shape, layer),
            _resident(pool_scale.shape, layer),
            _resident(rb.shape, layer),
        ],
        out_specs=[pl.BlockSpec((TM, POOL_W), lambda j: (jnp.minimum(j, n_tiles - 1), 0)),
                   pl.BlockSpec((TM, ATT_W), lambda j: (jnp.maximum(j - 1, 0), 0))],
        out_shape=[jax.ShapeDtypeStruct((t, POOL_W), _BF16), jax.ShapeDtypeStruct((t, ATT_W), _BF16)],
        scratch_shapes=[
            pltpu.VMEM((POOL_HALO + TM, POOL_W), _F32),
            q_buf, q_buf,
            kt_buf, kt_buf, kt_buf,
            v_buf, v_buf, v_buf,
            pltpu.VMEM((N_PAIRS, 2 * QSUB, KWIN), _F32),
            pltpu.VMEM((2, 2 * QSUB, KWIN), _F32),
            pltpu.VMEM((2, 2 * QSUB, LANES), _F32),
        ],
        compiler_params=pltpu.CompilerParams(
            dimension_semantics=("arbitrary",), vmem_limit_bytes=VMEM_LIMIT),
        name="mix_attn",
    )(h, g, w_in, w_kt, pool_w, pool_b, pool_scale, rb)


def _ff_chunks(d_ff):
    return tuple((off, min(FF_CHUNK, d_ff - off)) for off in range(0, d_ff, FF_CHUNK))


def _ffn_kernel(tiles_per_seq, d_ff, final, h_ref, a_ref, o_ref, p_ref, wout_ref, gf_ref, wup_ref, cw_ref,
                cb_ref, wdn_ref, gp_ref, wpg_ref, bpg_ref, wple_ref, fg_ref, out_ref,
                h1_s, hn_s, acc_s, gate_s, val_s, halo_s):
    @pl.when(pl.program_id(0) == 0)
    def _():
        halo_s[...] = jnp.zeros_like(halo_s)

    first = (pl.program_id(0) % tiles_per_seq) == 0
    chunks = _ff_chunks(d_ff)
    row_blocks = [slice(r, r + ROW_BLOCK) for r in range(0, TM, ROW_BLOCK)]

    def up(c, rows=slice(0, TM)):
        off, width = chunks[c]
        hn = hn_s[rows, :]
        gate_s[c % 2, CONV_HALO + rows.start:CONV_HALO + rows.stop, 0:width] = (
            _dot(hn, wup_ref[:, off:off + width]))
        val_s[c % 2, rows, 0:width] = _dot(hn, wup_ref[:, d_ff + off:d_ff + off + width])

    for rows in row_blocks:
        h1 = (h_ref[rows, :] + _dot(a_ref[rows, :], wout_ref[0:POOL_W, :])
              + _dot(o_ref[rows, :], wout_ref[POOL_W:POOL_W + ATT_W, :]))
        h1_s[rows, :] = h1
        hn_s[rows, :] = _rmsnorm(h1, gf_ref[...]).astype(_BF16)
    for rows in row_blocks:
        up(0, rows)

    def down(c):
        off, width = chunks[c]
        gate = gate_s.at[c % 2]
        gate[0:CONV_HALO, 0:width] = jnp.where(first, 0.0, halo_s[:, off:off + width])
        halo_s[:, off:off + width] = gate[TM:TM + CONV_HALO, 0:width]
        y = gate[CONV_HALO:CONV_HALO + TM, 0:width] * cw_ref[CONV_K - 1:CONV_K, off:off + width]
        y = y + cb_ref[:, off:off + width]
        for kk in range(CONV_K - 1):
            lag = CONV_K - 1 - kk
            y = y + gate[CONV_HALO - lag:CONV_HALO - lag + TM, 0:width] * cw_ref[kk:kk + 1, off:off + width]
        act = 0.5 * y * (1.0 + lax.erf(y * (1.0 / math.sqrt(2.0))))
        gated = (act * val_s[c % 2, :, 0:width]).astype(_BF16)
        for rows in row_blocks:
            part = _dot(gated[rows, :], wdn_ref[off:off + width, :])
            if c == 0:
                acc_s[rows, :] = part
            else:
                acc_s[rows, :] += part

    for c in range(len(chunks)):
        if c + 1 < len(chunks):
            for rows in row_blocks:
                up(c + 1, rows)
        down(c)

    for rows in row_blocks:
        h2 = h1_s[rows, :] + acc_s[rows, :]
        hn3 = _rmsnorm(h2, gp_ref[...]).astype(_BF16)
        gate = jax.nn.sigmoid(_dot(hn3, wpg_ref[...]) + bpg_ref[...])
        h3 = h2 + gate * _dot(p_ref[rows, :].astype(_BF16), wple_ref[...])
        out_ref[rows, :] = _rmsnorm(h3, fg_ref[...]) if final else h3


def _ffn(h, a, o, layer, p, w_out, g_ffn, w_up, conv_w, conv_b, w_down, g_ple, w_pg, b_pg, w_ple, final_g,
         tiles_per_seq, final):
    t, d = h.shape
    d_ff = w_down.shape[1]
    per_layer = (w_out, g_ffn, w_up, conv_w, conv_b, w_down, g_ple, w_pg, b_pg, w_ple)
    return pl.pallas_call(
        functools.partial(_ffn_kernel, tiles_per_seq, d_ff, final),
        grid=(t // TM,),
        in_specs=[_rows(d), _rows(POOL_W), _rows(ATT_W),
                  pl.BlockSpec((None, TM, p.shape[2]), lambda j: (layer, j, 0))]
        + [_resident(w.shape, layer) for w in per_layer] + [_resident(final_g.shape)],
        out_specs=_rows(d),
        out_shape=jax.ShapeDtypeStruct((t, d), _F32),
        scratch_shapes=[
            pltpu.VMEM((TM, d), _F32),
            pltpu.VMEM((TM, d), _BF16),
            pltpu.VMEM((TM, d), _F32),
            pltpu.VMEM((2, CONV_HALO + TM, FF_CHUNK), _F32),
            pltpu.VMEM((2, TM, FF_CHUNK), _F32),
            pltpu.VMEM((CONV_HALO, d_ff), _F32),
        ],
        compiler_params=pltpu.CompilerParams(
            dimension_semantics=("arbitrary",), vmem_limit_bytes=VMEM_LIMIT),
        name="ffn",
    )(h, a, o, p, *per_layer, final_g)


def kernel(x, p, norm_mix_g, w_in, pool_w, pool_b, pool_scale, rel_bias, w_out, norm_ffn_g, w_up,
           conv_w, conv_b, w_down, norm_ple_g, w_ple_gate, b_ple_gate, w_ple, final_g):
    b, s, d = x.shape
    depth = w_in.shape[0]
    d_ff = w_down.shape[1]
    assert s % TM == 0 and d_ff % LANES == 0 and w_up.shape[2] == 2 * d_ff
    assert w_in.shape[2] == POOL_W + 3 * ATT_W and w_out.shape[1] == POOL_W + ATT_W
    assert rel_bias.shape[1:] == (N_HEADS, 2 * MAX_REL + 1)
    t = b * s
    tiles_per_seq = s // TM

    w_in_b = w_in.astype(_BF16)
    w_k = lax.optimization_barrier(w_in[:, :, POOL_W + ATT_W:POOL_W + 2 * ATT_W])
    w_kt_b = w_k.transpose(0, 2, 1).astype(_BF16)
    pool_w_b = pool_w.astype(_BF16)
    w_out_b = w_out.astype(_BF16)
    w_up_b = w_up.astype(_BF16)
    w_down_b = w_down.astype(_BF16)
    w_pg_b = w_ple_gate.astype(_BF16)
    w_ple_b = w_ple.astype(_BF16)
    rb = rel_bias.astype(_F32)[:, :, :0:-1].reshape(depth, N_HEADS, 1, BAND)
    rows = lambda v: v.reshape(depth, 1, -1)
    p_rows = p.reshape(depth, t, -1)

    h = x.reshape(t, d)
    for i in range(depth):
        a, o = _mix_attn(h, i, rows(norm_mix_g), w_in_b, w_kt_b, pool_w_b, rows(pool_b), rows(pool_scale), rb,
                         tiles_per_seq)
        h = _ffn(h, a, o, i, p_rows, w_out_b, rows(norm_ffn_g), w_up_b, conv_w, rows(conv_b), w_down_b,
                 rows(norm_ple_g), w_pg_b, rows(b_ple_gate), w_ple_b, final_g.reshape(1, -1), tiles_per_seq,
                 final=(i == depth - 1))
    return h.reshape(b, s, d)
```

```python
import functools
import math

import jax
import jax.numpy as jnp
from jax import lax
from jax.experimental import pallas as pl
from jax.experimental.pallas import tpu as pltpu

CHUNK = 64
POOL_WINDOWS = (2, 4, 8, 16)
POOL_GROUP = 128
POOL_W = POOL_GROUP * len(POOL_WINDOWS)
N_HEADS = 8
HEAD_DIM = 64
ATT_W = N_HEADS * HEAD_DIM
LEFT_CHUNKS = 8
LEFT = LEFT_CHUNKS * CHUNK
MAX_REL = 128
CONV_K = 3
EPS = 1e-6

LANES = 128
SUBLANES = 8
TM = LEFT
QSUB = 2 * CHUNK
KWIN = QSUB + LEFT
N_PAIRS = N_HEADS * HEAD_DIM // LANES
N_SUBS = TM // QSUB
BAND = 2 * MAX_REL
ROW_BLOCK = 256
COL_BLOCK = 256
POOL_HALO = 16
CONV_HALO = SUBLANES
FF_CHUNK = 768
NEG = -1e30
LOG2E = math.log2(math.e)
VMEM_LIMIT = 56 * 1024 * 1024

_BF16 = jnp.bfloat16
_F32 = jnp.float32


def _dot(a, b):
    return jnp.dot(a, b, preferred_element_type=_F32)


def _dot_t(a, b):
    return lax.dot_general(a, b, (((1,), (1,)), ((), ())), preferred_element_type=_F32)


def _rmsnorm(x, g):
    ms = jnp.mean(x * x, axis=-1, keepdims=True)
    return x * lax.rsqrt(ms + EPS) * g


def _resident(shape, layer=None):
    if layer is None:
        return pl.BlockSpec(shape, lambda j: (0,) * len(shape), pipeline_mode=pl.Buffered(1))
    return pl.BlockSpec((None,) + tuple(shape[1:]), lambda j: (layer,) + (0,) * (len(shape) - 1),
                        pipeline_mode=pl.Buffered(1))


def _rows(width):
    return pl.BlockSpec((TM, width), lambda j: (j, 0))


def _interleave(xs, ys):
    out, taken = [], 0
    for i, x in enumerate(xs):
        out.append(x)
        want = (i + 1) * len(ys) // len(xs)
        out += ys[taken:want]
        taken = want
    return out


def _mix_in_pieces(tile, h_ref, g_ref, w_ref, wkt_ref, pw_ref, pb_ref, ps_ref, a_ref, ubuf, q_new, kt_new, v_new):
    row_blocks = [slice(r, r + ROW_BLOCK) for r in range(0, TM, ROW_BLOCK)]
    hn = {}

    def norm(rows):
        hn[rows.start] = _rmsnorm(h_ref[rows, :], g_ref[...]).astype(_BF16)

    def proj_u(rows, cols):
        ubuf[POOL_HALO + rows.start:POOL_HALO + rows.stop, cols] = _dot(hn[rows.start], w_ref[:, cols])

    def proj_q(rows, cols):
        q = _dot(hn[rows.start], w_ref[:, POOL_W + cols.start:POOL_W + cols.stop]) * (LOG2E / math.sqrt(HEAD_DIM))
        q_new[rows, cols] = q.astype(_BF16)

    def proj_k(rows, cols):
        kt_new[cols, rows] = _dot_t(wkt_ref[cols, :], hn[rows.start]).astype(_BF16)

    def proj_v(rows, cols):
        base = POOL_W + 2 * ATT_W
        v = _dot(hn[rows.start], w_ref[:, base + cols.start:base + cols.stop]).astype(_BF16)
        for i in range(COL_BLOCK // LANES):
            pair = cols.start // LANES + i
            v_new[rows, 2 * pair * LANES:(2 * pair + 1) * LANES] = v[:, i * LANES:(i + 1) * LANES]
            v_new[rows, (2 * pair + 1) * LANES:(2 * pair + 2) * LANES] = jnp.ones((ROW_BLOCK, LANES), _BF16)

    def pool(gi):
        win = POOL_WINDOWS[gi]
        cols = slice(gi * POOL_GROUP, (gi + 1) * POOL_GROUP)
        u = ubuf[POOL_HALO:POOL_HALO + TM, cols]
        s = u
        for d in range(1, win):
            s = s + ubuf[POOL_HALO - d:POOL_HALO - d + TM, cols]
        pos = tile * TM + lax.broadcasted_iota(jnp.int32, (TM, 1), 0)
        cnt = jnp.minimum(pos + 1, win).astype(_F32)
        y = (s / cnt - u).astype(_BF16)
        r = _dot(y, pw_ref[gi]) + pb_ref[:, cols]
        a_ref[:, cols] = (r * ps_ref[:, cols]).astype(_BF16)

    part = functools.partial
    col_blocks = [slice(c, c + COL_BLOCK) for c in range(0, ATT_W, COL_BLOCK)]
    norms = [part(norm, rows) for rows in row_blocks]
    projections = [part(proj, rows, cols) for proj in (proj_u, proj_q, proj_k, proj_v)
                   for rows in row_blocks for cols in col_blocks]
    pools = [part(pool, gi) for gi in range(len(POOL_WINDOWS))]
    return norms, projections, pools


def _build_bias_tables(rb_ref, tbl):
    row = lax.broadcasted_iota(jnp.int32, (QSUB, KWIN), 0)
    col = lax.broadcasted_iota(jnp.int32, (QSUB, KWIN), 1)
    first_key = row & ~(CHUNK - 1)
    visible = (col >= first_key) & (col < first_key + (LEFT_CHUNKS + 1) * CHUNK)
    brow = lax.broadcasted_iota(jnp.int32, (QSUB, BAND), 0)
    bcol = lax.broadcasted_iota(jnp.int32, (QSUB, BAND), 1)
    for head in range(N_HEADS):
        r = rb_ref[head]
        far = r[:, 0:1]
        rolled = pltpu.roll(jnp.broadcast_to(r, (QSUB, BAND)), 0, 1, stride=1, stride_axis=0)
        band = jnp.where(bcol >= brow, rolled, far)
        full = jnp.concatenate([jnp.broadcast_to(far, (QSUB, KWIN - BAND)), band], axis=1)
        tbl[head // 2, (head % 2) * QSUB:(head % 2 + 1) * QSUB, :] = jnp.where(visible, full * LOG2E, NEG)


def _attention_pieces(with_prev, q_ref, ktp_ref, ktc_ref, vp_ref, vc_ref, tbl, s_s, m_s, o_ref):
    low_head = lax.broadcasted_iota(jnp.int32, (1, LANES), 1) < HEAD_DIM
    n_blocks = N_PAIRS * N_SUBS

    def geometry(n):
        pair, sub = divmod(n, N_SUBS)
        n_prev = TM - sub * QSUB
        return pair, slice(sub * QSUB, (sub + 1) * QSUB), slice(pair * LANES, (pair + 1) * LANES), n_prev

    def scores(n):
        pair, rows, cols, n_prev = geometry(n)
        q2 = q_ref[rows, cols]
        zero = jnp.zeros_like(q2)
        qq = jnp.concatenate([jnp.where(low_head, q2, zero), jnp.where(low_head, zero, q2)], axis=0)
        s_cur = _dot(qq, ktc_ref[cols, 0:KWIN - n_prev]) + tbl[pair, :, n_prev:KWIN]
        s_s[n % 2, :, n_prev:KWIN] = s_cur
        m = jnp.max(s_cur, axis=-1, keepdims=True)
        if with_prev:
            s_prev = _dot(qq, ktp_ref[cols, TM - n_prev:TM]) + tbl[pair, :, 0:n_prev]
            s_s[n % 2, :, 0:n_prev] = s_prev
            m = jnp.maximum(m, jnp.max(s_prev, axis=-1, keepdims=True))
        m_s[n % 2] = jnp.broadcast_to(m, (2 * QSUB, LANES))

    e = {}

    def weights(n):
        _, _, _, n_prev = geometry(n)
        m = m_s[n % 2]

        def exps(lo, hi):
            return jnp.concatenate([jnp.exp2(s_s[n % 2, :, c:c + LANES] - m).astype(_BF16)
                                    for c in range(lo, hi, LANES)], axis=1)

        e[n] = (exps(0, n_prev) if with_prev else None, exps(n_prev, KWIN))

    def outputs(n):
        pair, rows, cols, n_prev = geometry(n)
        vcols = slice(2 * pair * LANES, (2 * pair + 2) * LANES)
        e_prev, e_cur = e.pop(n)
        ov = _dot(e_cur, vc_ref[0:KWIN - n_prev, vcols])
        if with_prev:
            ov = ov + _dot(e_prev, vp_ref[TM - n_prev:TM, vcols])
        o = ov[:, 0:LANES] / ov[:, LANES:2 * LANES]
        o_ref[rows, cols] = jnp.where(low_head, o[0:QSUB], o[QSUB:2 * QSUB]).astype(_BF16)

    part = functools.partial
    return [(part(scores, n), part(weights, n), part(outputs, n)) for n in range(n_blocks)]


def _mix_attn_kernel(tiles_per_seq, n_tiles, h_ref, g_ref, w_ref, wkt_ref, pw_ref, pb_ref, ps_ref, rb_ref,
                     a_ref, o_ref,
                     ubuf, q_cur, q_new, kt_prev, kt_cur, kt_new, v_prev, v_cur, v_new, tbl, s_s, m_s):
    step = pl.program_id(0)

    @pl.when(step == 0)
    def _():
        _build_bias_tables(rb_ref, tbl)
        ubuf[...] = jnp.zeros_like(ubuf)

    tile = step % tiles_per_seq

    def schedule(mix, attend, with_prev):
        norms, projections, pools, blocks = [], [], [], []
        if attend:
            if with_prev:
                kt_prev[...] = kt_cur[...]
                v_prev[...] = v_cur[...]
            kt_cur[...] = kt_new[...]
            v_cur[...] = v_new[...]
            q_cur[...] = q_new[...]
            blocks = _attention_pieces(with_prev, q_cur, kt_prev, kt_cur, v_prev, v_cur, tbl, s_s, m_s, o_ref)
        if mix:
            ubuf[0:POOL_HALO, :] = jnp.where(tile == 0, 0.0, ubuf[TM:TM + POOL_HALO, :])
            norms, projections, pools = _mix_in_pieces(tile, h_ref, g_ref, w_ref, wkt_ref, pw_ref, pb_ref,
                                                       ps_ref, a_ref, ubuf, q_new, kt_new, v_new)
        for run in norms:
            run()
        n_u = len(projections) // 4
        queue = projections[:n_u] + _interleave(projections[n_u:], pools)
        if not blocks:
            for run in queue:
                run()
            return
        blocks[0][0]()
        for n, (_, weights, outputs) in enumerate(blocks):
            weights()
            if n + 1 < len(blocks):
                blocks[n + 1][0]()
            for run in queue[n * len(queue) // len(blocks):(n + 1) * len(queue) // len(blocks)]:
                run()
            outputs()

    has_prev = ((step - 1) % tiles_per_seq) != 0
    middle = jnp.logical_and(step > 0, step < n_tiles)
    pl.when(step == 0)(functools.partial(schedule, True, False, False))
    pl.when(jnp.logical_and(middle, has_prev))(functools.partial(schedule, True, True, True))
    pl.when(jnp.logical_and(middle, jnp.logical_not(has_prev)))(functools.partial(schedule, True, True, False))
    pl.when(step == n_tiles)(functools.partial(schedule, False, True, (n_tiles - 1) % tiles_per_seq != 0))


def _mix_attn(h, layer, g, w_in, w_kt, pool_w, pool_b, pool_scale, rb, tiles_per_seq):
    t, d = h.shape
    n_tiles = t // TM
    kt_buf = pltpu.VMEM((ATT_W, TM), _BF16)
    v_buf = pltpu.VMEM((TM, 2 * ATT_W), _BF16)
    q_buf = pltpu.VMEM((TM, ATT_W), _BF16)
    return pl.pallas_call(
        functools.partial(_mix_attn_kernel, tiles_per_seq, n_tiles),
        grid=(n_tiles + 1,),
        in_specs=[
            pl.BlockSpec((TM, d), lambda j: (jnp.minimum(j, n_tiles - 1), 0)),
            _resident(g.shape, layer),
            _resident(w_in.shape, layer),
            _resident(w_kt.shape, layer),
            _resident(pool_w.shape, layer),
            _resident(pool_b.shape, layer),
            _resident(pool_scale.shape, layer),
            _resident(rb.shape, layer),
        ],
        out_specs=[pl.BlockSpec((TM, POOL_W), lambda j: (jnp.minimum(j, n_tiles - 1), 0)),
                   pl.BlockSpec((TM, ATT_W), lambda j: (jnp.maximum(j - 1, 0), 0))],
        out_shape=[jax.ShapeDtypeStruct((t, POOL_W), _BF16), jax.ShapeDtypeStruct((t, ATT_W), _BF16)],
        scratch_shapes=[
            pltpu.VMEM((POOL_HALO + TM, POOL_W), _F32),
            q_buf, q_buf,
            kt_buf, kt_buf, kt_buf,
            v_buf, v_buf, v_buf,
            pltpu.VMEM((N_PAIRS, 2 * QSUB, KWIN), _F32),
            pltpu.VMEM((2, 2 * QSUB, KWIN), _F32),
            pltpu.VMEM((2, 2 * QSUB, LANES), _F32),
        ],
        compiler_params=pltpu.CompilerParams(
            dimension_semantics=("arbitrary",), vmem_limit_bytes=VMEM_LIMIT),
        name="mix_attn",
    )(h, g, w_in, w_kt, pool_w, pool_b, pool_scale, rb)


def _ff_chunks(d_ff):
    return tuple((off, min(FF_CHUNK, d_ff - off)) for off in range(0, d_ff, FF_CHUNK))


def _ffn_kernel(tiles_per_seq, d_ff, final, h_ref, a_ref, o_ref, p_ref, wout_ref, gf_ref, wup_ref, cw_ref,
                cb_ref, wdn_ref, gp_ref, wpg_ref, bpg_ref, wple_ref, fg_ref, out_ref,
                h1_s, hn_s, acc_s, gate_s, val_s, halo_s):
    @pl.when(pl.program_id(0) == 0)
    def _():
        halo_s[...] = jnp.zeros_like(halo_s)

    first = (pl.program_id(0) % tiles_per_seq) == 0
    chunks = _ff_chunks(d_ff)
    row_blocks = [slice(r, r + ROW_BLOCK) for r in range(0, TM, ROW_BLOCK)]

    def up(c, rows=slice(0, TM)):
        off, width = chunks[c]
        hn = hn_s[rows, :]
        gate_s[c % 2, CONV_HALO + rows.start:CONV_HALO + rows.stop, 0:width] = (
            _dot(hn, wup_ref[:, off:off + width]))
        val_s[c % 2, rows, 0:width] = _dot(hn, wup_ref[:, d_ff + off:d_ff + off + width])

    for rows in row_blocks:
        h1 = (h_ref[rows, :] + _dot(a_ref[rows, :], wout_ref[0:POOL_W, :])
              + _dot(o_ref[rows, :], wout_ref[POOL_W:POOL_W + ATT_W, :]))
        h1_s[rows, :] = h1
        hn_s[rows, :] = _rmsnorm(h1, gf_ref[...]).astype(_BF16)
    for rows in row_blocks:
        up(0, rows)

    def down(c):
        off, width = chunks[c]
        gate = gate_s.at[c % 2]
        gate[0:CONV_HALO, 0:width] = jnp.where(first, 0.0, halo_s[:, off:off + width])
        halo_s[:, off:off + width] = gate[TM:TM + CONV_HALO, 0:width]
        y = gate[CONV_HALO:CONV_HALO + TM, 0:width] * cw_ref[CONV_K - 1:CONV_K, off:off + width]
        y = y + cb_ref[:, off:off + width]
        for kk in range(CONV_K - 1):
            lag = CONV_K - 1 - kk
            y = y + gate[CONV_HALO - lag:CONV_HALO - lag + TM, 0:width] * cw_ref[kk:kk + 1, off:off + width]
        act = 0.5 * y * (1.0 + lax.erf(y * (1.0 / math.sqrt(2.0))))
        gated = (act * val_s[c % 2, :, 0:width]).astype(_BF16)
        for rows in row_blocks:
            part = _dot(gated[rows, :], wdn_ref[off:off + width, :])
            if c == 0:
                acc_s[rows, :] = part
            else:
                acc_s[rows, :] += part

    for c in range(len(chunks)):
        if c + 1 < len(chunks):
            for rows in row_blocks:
                up(c + 1, rows)
        down(c)

    for rows in row_blocks:
        h2 = h1_s[rows, :] + acc_s[rows, :]
        hn3 = _rmsnorm(h2, gp_ref[...]).astype(_BF16)
        gate = jax.nn.sigmoid(_dot(hn3, wpg_ref[...]) + bpg_ref[...])
        h3 = h2 + gate * _dot(p_ref[rows, :].astype(_BF16), wple_ref[...])
        out_ref[rows, :] = _rmsnorm(h3, fg_ref[...]) if final else h3


def _ffn(h, a, o, layer, p, w_out, g_ffn, w_up, conv_w, conv_b, w_down, g_ple, w_pg, b_pg, w_ple, final_g,
         tiles_per_seq, final):
    t, d = h.shape
    d_ff = w_down.shape[1]
    per_layer = (w_out, g_ffn, w_up, conv_w, conv_b, w_down, g_ple, w_pg, b_pg, w_ple)
    return pl.pallas_call(
        functools.partial(_ffn_kernel, tiles_per_seq, d_ff, final),
        grid=(t // TM,),
        in_specs=[_rows(d), _rows(POOL_W), _rows(ATT_W),
                  pl.BlockSpec((None, TM, p.shape[2]), lambda j: (layer, j, 0))]
        + [_resident(w.shape, layer) for w in per_layer] + [_resident(final_g.shape)],
        out_specs=_rows(d),
        out_shape=jax.ShapeDtypeStruct((t, d), _F32),
        scratch_shapes=[
            pltpu.VMEM((TM, d), _F32),
            pltpu.VMEM((TM, d), _BF16),
            pltpu.VMEM((TM, d), _F32),
            pltpu.VMEM((2, CONV_HALO + TM, FF_CHUNK), _F32),
            pltpu.VMEM((2, TM, FF_CHUNK), _F32),
            pltpu.VMEM((CONV_HALO, d_ff), _F32),
        ],
        compiler_params=pltpu.CompilerParams(
            dimension_semantics=("arbitrary",), vmem_limit_bytes=VMEM_LIMIT),
        name="ffn",
    )(h, a, o, p, *per_layer, final_g)


def kernel(x, p, norm_mix_g, w_in, pool_w, pool_b, pool_scale, rel_bias, w_out, norm_ffn_g, w_up,
           conv_w, conv_b, w_down, norm_ple_g, w_ple_gate, b_ple_gate, w_ple, final_g):
    b, s, d = x.shape
    depth = w_in.shape[0]
    d_ff = w_down.shape[1]
    assert s % TM == 0 and d_ff % LANES == 0 and w_up.shape[2] == 2 * d_ff
    assert w_in.shape[2] == POOL_W + 3 * ATT_W and w_out.shape[1] == POOL_W + ATT_W
    assert rel_bias.shape[1:] == (N_HEADS, 2 * MAX_REL + 1)
    t = b * s
    tiles_per_seq = s // TM

    w_in_b = w_in.astype(_BF16)
    w_k = lax.optimization_barrier(w_in[:, :, POOL_W + ATT_W:POOL_W + 2 * ATT_W])
    w_kt_b = w_k.transpose(0, 2, 1).astype(_BF16)
    pool_w_b = pool_w.astype(_BF16)
    w_out_b = w_out.astype(_BF16)
    w_up_b = w_up.astype(_BF16)
    w_down_b = w_down.astype(_BF16)
    w_pg_b = w_ple_gate.astype(_BF16)
    w_ple_b = w_ple.astype(_BF16)
    rb = rel_bias.astype(_F32)[:, :, :0:-1].reshape(depth, N_HEADS, 1, BAND)
    rows = lambda v: v.reshape(depth, 1, -1)
    p_rows = p.reshape(depth, t, -1)

    h = x.reshape(t, d)
    for i in range(depth):
        a, o = _mix_attn(h, i, rows(norm_mix_g), w_in_b, w_kt_b, pool_w_b, rows(pool_b), rows(pool_scale), rb,
                         tiles_per_seq)
        h = _ffn(h, a, o, i, p_rows, w_out_b, rows(norm_ffn_g), w_up_b, conv_w, rows(conv_b), w_down_b,
                 rows(norm_ple_g), w_pg_b, rows(b_ple_gate), w_ple_b, final_g.reshape(1, -1), tiles_per_seq,
                 final=(i == depth - 1))
    return h.reshape(b, s, d)
```
